```python
import math
import jax, jax.numpy as jnp
from jax import lax
import numpy as np

D_MODEL = 2048
BATCH = 32
SEQ = 256
DEPTH = 2
DEC_BATCH = 4
DEC_SEQ = 1024
PAST_LEN = 256

GRID_W = 64
EPS = 1e-6
ROPE_BASE = 10000.0
RET_HEADS = 8
RET_DK = 64
RET_DV = 128
RET_QK = RET_HEADS * RET_DK
RET_W = RET_HEADS * RET_DV
RET_CHUNK = 128
HY_W = 1024
HY_ORDER = 2
HY_EMB = 33
HY_HID = 64
HY_SHORT = 3
HY_TARGET = 1e-2
HY_FAST = 0.3
HY_SLOW = 1.5
DN_HEADS = 8
DN_DK = 128
DN_DV = 128
DN_QK = DN_HEADS * DN_DK
DN_W = DN_HEADS * DN_DV
DN_CONV = 3
DN_CHUNK = 64
N_BRANCH = 3
FFN_DIM = 5504
FFN_CONV = 3
N_MOD = 6
SPLIT_SIZES = (RET_QK, RET_QK, RET_W, RET_W, (HY_ORDER + 1) * HY_W, 2 * DN_QK + DN_W, DN_W,
               2 * DN_HEADS, 2 * DN_HEADS, N_BRANCH * D_MODEL)
IN_COLS = 16416

kernel_name = 'hybrid_retention_hyena_deltanet_diffusion_step'


def _split_points(sizes):
    return [int(v) for v in np.cumsum(np.array(sizes))[:-1]]


def rms_normalize(x):
    xf = x.astype(jnp.float32)
    return xf * lax.rsqrt(jnp.mean(xf * xf, axis=-1, keepdims=True) + EPS)


def rmsnorm(x, w):
    return (rms_normalize(x) * w.astype(jnp.float32)).astype(x.dtype)


def l2norm(x):
    return x * lax.rsqrt(jnp.sum(x * x, axis=-1, keepdims=True) + EPS)


def dwconv(x, w):
    k, ch = w.shape
    return lax.conv_general_dilated(x, w[:, None, :].astype(x.dtype), window_strides=(1,),
                                    padding=[(k // 2, k // 2)],
                                    dimension_numbers=('NWC', 'WIO', 'NWC'),
                                    feature_group_count=ch)


def rope_2d(x):
    _, L, _, dk = x.shape
    n_rows = L // GRID_W
    pos_r = jnp.repeat(jnp.arange(n_rows, dtype=jnp.float32), GRID_W)
    pos_c = jnp.tile(jnp.arange(GRID_W, dtype=jnp.float32), n_rows)
    nf = dk // 4
    inv = ROPE_BASE ** (-jnp.arange(nf, dtype=jnp.float32) / nf)

    def rot(xh, pos):
        ang = pos[:, None] * inv[None, :]
        cos = jnp.cos(ang)[None, :, None, :]
        sin = jnp.sin(ang)[None, :, None, :]
        x1, x2 = xh[..., :nf], xh[..., nf:]
        return jnp.concatenate([x1 * cos - x2 * sin, x1 * sin + x2 * cos], axis=-1)

    half = dk // 2
    return jnp.concatenate([rot(x[..., :half], pos_r), rot(x[..., half:], pos_c)], axis=-1)


def retention_scan(q, k, v, log_g, s0):
    B, L, H, _ = q.shape
    dv = v.shape[-1]
    C = RET_CHUNK
    n = L // C

    def blk(t):
        return jnp.moveaxis(t.reshape(B, n, C, H, t.shape[-1]), (1, 3), (0, 2))

    idx = jnp.arange(C, dtype=jnp.float32)
    rel = idx[:, None] - idx[None, :]
    causal = rel >= 0
    dmat = jnp.where(causal, jnp.exp(jnp.where(causal, rel, 0.0)[None] * log_g[:, None, None]), 0.0)
    q_dec = jnp.exp((idx + 1.0)[None, :] * log_g[:, None])[:, :, None]
    k_dec = jnp.exp((C - 1.0 - idx)[None, :] * log_g[:, None])[:, :, None]
    c_dec = jnp.exp(C * log_g)[:, None, None]

    def step(s, inp):
        qc, kc, vc = inp
        scores = jnp.einsum('bhid,bhjd->bhij', qc, kc) * dmat
        o = (jnp.einsum('bhij,bhjv->bhiv', scores, vc)
             + jnp.einsum('bhid,bhdv->bhiv', qc * q_dec, s))
        s = s * c_dec + jnp.einsum('bhjd,bhjv->bhdv', kc * k_dec, vc)
        return s, o

    s_fin, o = lax.scan(step, s0, (blk(q), blk(k), blk(v)))
    o = jnp.moveaxis(o, (0, 2), (1, 3)).reshape(B, L, H, dv)
    return o, s_fin


def gated_delta_scan(q, k, v, g, beta, s0):
    B, L, H, _ = q.shape
    dv = v.shape[-1]
    C = DN_CHUNK
    n = L // C

    def blk(t):
        return jnp.moveaxis(t.reshape((B, n, C, H) + t.shape[3:]), 3, 1)

    q, k, v, g, beta = blk(q), blk(k), blk(v), blk(g), blk(beta)
    gc = jnp.cumsum(g, axis=-1)
    idx = jnp.arange(C)
    incl = idx[:, None] >= idx[None, :]
    strict = idx[:, None] > idx[None, :]
    decay = jnp.exp(jnp.where(incl, gc[..., :, None] - gc[..., None, :], -jnp.inf))
    kb = k * beta[..., None]
    a_mat = (jnp.where(strict, jnp.einsum('bhncd,bhnsd->bhncs', kb, k) * decay, 0.0)
             + jnp.eye(C, dtype=jnp.float32))
    rhs = jnp.concatenate([v * beta[..., None], kb * jnp.exp(gc)[..., None]], axis=-1)
    sol = lax.linalg.triangular_solve(a_mat, rhs, left_side=True, lower=True, unit_diagonal=True)
    u, w = sol[..., :dv], sol[..., dv:]
    attn = jnp.einsum('bhncd,bhnsd->bhncs', q, k) * decay
    g_last = gc[..., -1]
    qd = q * jnp.exp(gc)[..., None]
    kd = k * jnp.exp(g_last[..., None] - gc)[..., None]

    def step(s, inp):
        qd_c, kd_c, u_c, w_c, a_c, gl_c = inp
        v_new = u_c - jnp.einsum('bhcd,bhdv->bhcv', w_c, s)
        o = jnp.einsum('bhcd,bhdv->bhcv', qd_c, s) + jnp.einsum('bhcs,bhsv->bhcv', a_c, v_new)
        s = s * jnp.exp(gl_c)[..., None, None] + jnp.einsum('bhcd,bhcv->bhdv', kd_c, v_new)
        return s, o

    xs = (jnp.moveaxis(qd, 2, 0), jnp.moveaxis(kd, 2, 0), jnp.moveaxis(u, 2, 0),
          jnp.moveaxis(w, 2, 0), jnp.moveaxis(attn, 2, 0), jnp.moveaxis(g_last, 2, 0))
    s_fin, o = lax.scan(step, s0, xs)
    o = jnp.moveaxis(o, 0, 2).reshape(B, H, L, dv)
    return jnp.moveaxis(o, 1, 2), s_fin


def hyena_filters(L, w1, b1, fr1, w2, b2, fr2, w3):
    f32 = jnp.float32
    t = jnp.linspace(0.0, 1.0, L, dtype=f32)[:, None]
    bands = (HY_EMB - 1) // 2
    wpos = 2.0 * math.pi * jnp.arange(L, dtype=f32)[:, None] / L
    fr = jnp.linspace(1e-4, bands - 1, bands, dtype=f32)[None, :]
    feats = jnp.concatenate([t, jnp.cos(fr * wpos), -jnp.sin(fr * wpos)], axis=-1)
    hid = jnp.sin(fr1.astype(f32) * (feats @ w1.astype(f32) + b1.astype(f32)))
    hid = jnp.sin(fr2.astype(f32) * (hid @ w2.astype(f32) + b2.astype(f32)))
    h = (hid @ w3.astype(f32)).reshape(L, 2, HY_ORDER, HY_W)
    deltas = jnp.abs(jnp.linspace(math.log(HY_TARGET) / HY_FAST, math.log(HY_TARGET) / HY_SLOW,
                                  HY_W, dtype=f32))
    h = h * jnp.exp(-t * deltas[None, :])[:, None, None, :]
    return h / (jnp.sum(jnp.abs(h), axis=0, keepdims=True) + EPS)


def fft_long_conv(z, h_f, h_b, bias):
    L = z.shape[1]
    n = 2 * L
    zf = z.astype(jnp.float32)
    zs = jnp.fft.rfft(zf, n=n, axis=1)
    hs = jnp.fft.rfft(h_f, n=n, axis=0) + jnp.conj(jnp.fft.rfft(h_b, n=n, axis=0))
    y = jnp.fft.irfft(zs * hs[None], n=n, axis=1)[:, :L]
    return (y + zf * bias.astype(jnp.float32)).astype(z.dtype)


def hyena(u, filt, bias, w_short):
    u = dwconv(u, w_short)
    v, x1, x2 = jnp.split(u, 3, axis=-1)
    z = x1 * fft_long_conv(v, filt[:, 0, 0], filt[:, 1, 0], bias[0])
    z = x2 * fft_long_conv(z, filt[:, 0, 1], filt[:, 1, 1], bias[1])
    return z


def trunk_layer(x, mod, s_ret0, s_dn0, latent, p):
    f32 = jnp.float32
    B, L, _ = x.shape
    fl = lambda t: jnp.flip(t, axis=1)
    sh1, sc1, g1, sh2, sc2, g2 = jnp.split(mod, N_MOD, axis=-1)
    h = rmsnorm(x, p['norm1']) * (1 + sc1) + sh1
    proj = h @ p['w_in']
    rq, rk, rv, rg, hy, dqkv, dz, da, db, mg = jnp.split(proj, _split_points(SPLIT_SIZES), axis=-1)

    q = rq.reshape(B, L, RET_HEADS, RET_DK).astype(f32)
    k = rk.reshape(B, L, RET_HEADS, RET_DK).astype(f32) * (RET_DK ** -0.5)
    if latent:
        q, k = rope_2d(q), rope_2d(k)
    v = rv.reshape(B, L, RET_HEADS, RET_DV).astype(f32)
    lg = jax.nn.log_sigmoid(p['ret_decay'].astype(f32))
    o_f, sr_f = retention_scan(q, k, v, lg[0], s_ret0[:, 0])
    o_b, sr_b = retention_scan(fl(q), fl(k), fl(v), lg[1], s_ret0[:, 1])
    o_r = rms_normalize(o_f + fl(o_b)).reshape(B, L, RET_W)
    ret_out = (o_r * jax.nn.silu(rg.astype(f32))).astype(x.dtype)

    filt = hyena_filters(L, p['hy_w1'], p['hy_b1'], p['hy_freq1'], p['hy_w2'], p['hy_b2'],
                         p['hy_freq2'], p['hy_w3'])
    hy_out = hyena(hy, filt, p['hy_bias'], p['hy_short'])

    qkv = jax.nn.silu(dwconv(dqkv, p['dn_conv'])).astype(f32)
    q2, k2, v2 = jnp.split(qkv, [DN_QK, 2 * DN_QK], axis=-1)
    q2 = l2norm(q2.reshape(B, L, DN_HEADS, DN_DK)) * (DN_DK ** -0.5)
    k2 = l2norm(k2.reshape(B, L, DN_HEADS, DN_DK))
    v2 = v2.reshape(B, L, DN_HEADS, DN_DV)
    beta = jax.nn.sigmoid(db.astype(f32)).reshape(B, L, 2, DN_HEADS)
    gdec = (-jnp.exp(p['dn_a_log'].astype(f32))
            * jax.nn.softplus(da.astype(f32).reshape(B, L, 2, DN_HEADS) + p['dn_dt_bias'].astype(f32)))
    od_f, sd_f = gated_delta_scan(q2, k2, v2, gdec[:, :, 0], beta[:, :, 0], s_dn0[:, 0])
    od_b, sd_b = gated_delta_scan(fl(q2), fl(k2), fl(v2), fl(gdec[:, :, 1]), fl(beta[:, :, 1]),
                                  s_dn0[:, 1])
    o_d = (rms_normalize(od_f + fl(od_b)) * p['dn_norm'].astype(f32)).reshape(B, L, DN_W)
    dn_out = (o_d * jax.nn.silu(dz.astype(f32))).astype(x.dtype)

    gr, gh, gd = jnp.split(jax.nn.sigmoid(mg), N_BRANCH, axis=-1)
    mix = gr * (ret_out @ p['p_ret']) + gh * (hy_out @ p['p_hy']) + gd * (dn_out @ p['p_dn'])
    x = x + g1 * (mix @ p['w_o'])

    h2 = rmsnorm(x, p['norm2']) * (1 + sc2) + sh2
    up = dwconv(h2 @ p['w_up'], p['ffn_conv'])
    ga, gb = jnp.split(up, 2, axis=-1)
    x = x + g2 * ((jax.nn.silu(ga) * gb) @ p['w_down'])
    return x, jnp.stack([sr_f, sr_b], axis=1), jnp.stack([sd_f, sd_b], axis=1)


def setup_inputs(seed: int = 0) -> dict:
    key = jax.random.key(seed)
    ks = jax.random.split(key, 40)
    f32 = jnp.float32
    nrm = lambda k, shape, s: jax.random.normal(k, shape, f32) * s
    D = D_MODEL
    gam = 1.0 - 2.0 ** (-5.0 - jnp.arange(RET_HEADS, dtype=f32))
    ret_logit = jnp.log(gam) - jnp.log1p(-gam)
    dt = jnp.exp(jax.random.uniform(ks[20], (DEPTH, 2, DN_HEADS), f32,
                                    minval=math.log(1e-3), maxval=math.log(1e-1)))
    return {
        'x_prompt': nrm(ks[0], (BATCH, SEQ, D), 1.0),
        'x_sample': nrm(ks[1], (DEC_BATCH, DEC_SEQ, D), 1.0),
        'state_ret': nrm(ks[2], (DEC_BATCH, DEPTH, 2, RET_HEADS, RET_DK, RET_DV), 0.5),
        'state_dn': nrm(ks[3], (DEC_BATCH, DEPTH, 2, DN_HEADS, DN_DK, DN_DV), 0.5),
        'c': nrm(ks[4], (DEC_BATCH, D), 1.0),
        'c_ctx': nrm(ks[5], (D,), 1.0),
        'w_ada': nrm(ks[6], (DEPTH, D, N_MOD * D), 0.5 * D ** -0.5),
        'b_ada': nrm(ks[7], (DEPTH, N_MOD * D), 0.02),
        'norm1': 1.0 + nrm(ks[8], (DEPTH, D), 0.02),
        'w_in': nrm(ks[9], (DEPTH, D, IN_COLS), D ** -0.5),
        'ret_decay': ret_logit[None, None, :] + nrm(ks[10], (DEPTH, 2, RET_HEADS), 0.1),
        'hy_short': nrm(ks[11], (DEPTH, HY_SHORT, (HY_ORDER + 1) * HY_W), HY_SHORT ** -0.5),
        'hy_w1': nrm(ks[12], (DEPTH, HY_EMB, HY_HID), HY_EMB ** -0.5),
        'hy_b1': nrm(ks[13], (DEPTH, HY_HID), 0.02),
        'hy_freq1': 1.0 + nrm(ks[14], (DEPTH, HY_HID), 0.02),
        'hy_w2': nrm(ks[15], (DEPTH, HY_HID, HY_HID), HY_HID ** -0.5),
        'hy_b2': nrm(ks[16], (DEPTH, HY_HID), 0.02),
        'hy_freq2': 1.0 + nrm(ks[17], (DEPTH, HY_HID), 0.02),
        'hy_w3': nrm(ks[18], (DEPTH, HY_HID, 2 * HY_ORDER * HY_W), HY_HID ** -0.5),
        'hy_bias': nrm(ks[19], (DEPTH, HY_ORDER, HY_W), 0.1),
        'dn_conv': nrm(ks[21], (DEPTH, DN_CONV, 2 * DN_QK + DN_W), DN_CONV ** -0.5),
        'dn_a_log': jnp.log(jax.random.uniform(ks[22], (DEPTH, 2, DN_HEADS), f32, minval=1.0, maxval=16.0)),
        'dn_dt_bias': dt + jnp.log(-jnp.expm1(-dt)),
        'dn_norm': 1.0 + nrm(ks[23], (DEPTH, DN_DV), 0.02),
        'p_ret': nrm(ks[24], (DEPTH, RET_W, D), RET_W ** -0.5),
        'p_hy': nrm(ks[25], (DEPTH, HY_W, D), HY_W ** -0.5),
        'p_dn': nrm(ks[26], (DEPTH, DN_W, D), DN_W ** -0.5),
        'w_o': nrm(ks[27], (DEPTH, D, D), D ** -0.5),
        'norm2': 1.0 + nrm(ks[28], (DEPTH, D), 0.02),
        'w_up': nrm(ks[29], (DEPTH, D, 2 * FFN_DIM), D ** -0.5),
        'ffn_conv': nrm(ks[30], (DEPTH, FFN_CONV, 2 * FFN_DIM), FFN_CONV ** -0.5),
        'w_down': nrm(ks[31], (DEPTH, FFN_DIM, D), FFN_DIM ** -0.5),
        'norm_f': 1.0 + nrm(ks[32], (D,), 0.02),
    }


def reference(x_prompt, x_sample, state_ret, state_dn, c, c_ctx, w_ada, b_ada, norm1, w_in,
              ret_decay, hy_short, hy_w1, hy_b1, hy_freq1, hy_w2, hy_b2, hy_freq2, hy_w3, hy_bias,
              dn_conv, dn_a_log, dn_dt_bias, dn_norm, p_ret, p_hy, p_dn, w_o, norm2, w_up,
              ffn_conv, w_down, norm_f):
    f32 = jnp.float32
    xp, xs = x_prompt, x_sample
    bp = xp.shape[0]
    zero_ret = jnp.zeros((bp, 2, RET_HEADS, RET_DK, RET_DV), f32)
    zero_dn = jnp.zeros((bp, 2, DN_HEADS, DN_DK, DN_DV), f32)
    new_ret, new_dn = [], []
    for l in range(DEPTH):
        p = dict(norm1=norm1[l], w_in=w_in[l], ret_decay=ret_decay[l], hy_short=hy_short[l],
                 hy_w1=hy_w1[l], hy_b1=hy_b1[l], hy_freq1=hy_freq1[l], hy_w2=hy_w2[l],
                 hy_b2=hy_b2[l], hy_freq2=hy_freq2[l], hy_w3=hy_w3[l], hy_bias=hy_bias[l],
                 dn_conv=dn_conv[l], dn_a_log=dn_a_log[l], dn_dt_bias=dn_dt_bias[l],
                 dn_norm=dn_norm[l], p_ret=p_ret[l], p_hy=p_hy[l], p_dn=p_dn[l], w_o=w_o[l],
                 norm2=norm2[l], w_up=w_up[l], ffn_conv=ffn_conv[l], w_down=w_down[l])
        mod_ctx = (jax.nn.silu(c_ctx) @ w_ada[l] + b_ada[l])[None, None, :]
        mod_lat = (jax.nn.silu(c) @ w_ada[l] + b_ada[l])[:, None, :]
        xp, s_r, s_d = trunk_layer(xp, mod_ctx, zero_ret, zero_dn, False, p)
        new_ret.append(s_r)
        new_dn.append(s_d)
        xs, _, _ = trunk_layer(xs, mod_lat, state_ret[:, l].astype(f32), state_dn[:, l].astype(f32),
                               True, p)
    y_prompt = rmsnorm(xp, norm_f)
    y_sample = rmsnorm(xs, norm_f)
    new_state_ret = jnp.stack(new_ret, axis=1).astype(x_prompt.dtype)
    new_state_dn = jnp.stack(new_dn, axis=1).astype(x_prompt.dtype)
    return (y_prompt, y_sample, new_state_ret, new_state_dn)
```

```python
import functools
import math

import jax
import jax.numpy as jnp
import numpy as np
from jax import lax
from jax.experimental import pallas as pl
from jax.experimental.pallas import tpu as pltpu

F32 = jnp.float32
BF16 = jnp.bfloat16

D_MODEL = 2048
BATCH = 32
SEQ = 256
DEPTH = 2
DEC_BATCH = 4
DEC_SEQ = 1024
GRID_W = 64
EPS = 1e-6
ROPE_BASE = 10000.0
RET_HEADS = 8
RET_DK = 64
RET_DV = 128
RET_QK = RET_HEADS * RET_DK
RET_W = RET_HEADS * RET_DV
RET_CHUNK = 128
HY_W = 1024
HY_ORDER = 2
HY_EMB = 33
HY_HID = 64
HY_TARGET = 1e-2
HY_FAST = 0.3
HY_SLOW = 1.5
DN_HEADS = 8
DN_DK = 128
DN_DV = 128
DN_QK = DN_HEADS * DN_DK
DN_W = DN_HEADS * DN_DV
DN_CHUNK = 64
N_BRANCH = 3
FFN_DIM = 5504
N_MOD = 6

MP = BATCH * SEQ
MS = DEC_BATCH * DEC_SEQ
M_TOK = MP + MS

LANES = 128
TM = 1024
TN = 512
FFN_PAD = 5632
VMEM_LIMIT = 56 * 1024 * 1024

C_RQ, C_RK, C_RV, C_RG = 0, 512, 1024, 2048
C_HY, C_DQKV, C_DZ, C_MG = 3072, 6144, 9216, 10240
N_MAIN = 16384
O_DA, O_MG = 10240, 10272


def _cparams(sem):
    return pltpu.CompilerParams(dimension_semantics=sem, vmem_limit_bytes=VMEM_LIMIT)


def _mod_row(i):
    return jnp.where(i < MP // TM, 0, 1 + (i - MP // TM) // (DEC_SEQ // TM))


def _mod_kernel(cond_ref, w_ref, b_ref, o_ref):
    c = cond_ref[...]
    s = (c * jax.nn.sigmoid(c)).astype(BF16)
    o_ref[0] = jnp.dot(s, w_ref[0].astype(BF16), preferred_element_type=F32) + b_ref[0]


def _adaln(cond, w_ada, b_ada):
    tn = 1024
    return pl.pallas_call(
        _mod_kernel,
        grid=(DEPTH, N_MOD * D_MODEL // tn),
        in_specs=[pl.BlockSpec((8, D_MODEL), lambda l, j: (0, 0)),
                  pl.BlockSpec((1, D_MODEL, tn), lambda l, j: (l, 0, j)),
                  pl.BlockSpec((1, 1, tn), lambda l, j: (l, 0, j))],
        out_specs=pl.BlockSpec((1, 8, tn), lambda l, j: (l, 0, j)),
        out_shape=jax.ShapeDtypeStruct((DEPTH, 8, N_MOD * D_MODEL), F32),
        compiler_params=_cparams(("arbitrary", "arbitrary")),
        name="adaln_mod",
    )(cond, w_ada, b_ada.reshape(DEPTH, 1, N_MOD * D_MODEL))


def _modulated_norm(x, nw, sc, sh):
    var = jnp.mean(x * x, axis=-1, keepdims=True)
    return (x * lax.rsqrt(var + EPS) * nw) * (1.0 + sc) + sh


def _proj_in_kernel(x_ref, nw_ref, sh_ref, sc_ref, w_ref, wab_ref, o_ref, ab_ref, h_ref):
    @pl.when(pl.program_id(1) == 0)
    def _():
        h = _modulated_norm(x_ref[...], nw_ref[...], sc_ref[0], sh_ref[0]).astype(BF16)
        h_ref[...] = h
        ab_ref[...] = jnp.dot(h, wab_ref[...], preferred_element_type=F32)

    o_ref[...] = jnp.dot(h_ref[...], w_ref[...], preferred_element_type=F32)


def _proj_in(x, nw, mod, w_main, w_ab):
    nb = D_MODEL // D_MODEL
    return pl.pallas_call(
        _proj_in_kernel,
        grid=(M_TOK // TM, N_MAIN // TN),
        in_specs=[pl.BlockSpec((TM, D_MODEL), lambda i, j: (i, 0)),
                  pl.BlockSpec((1, D_MODEL), lambda i, j: (0, 0)),
                  pl.BlockSpec((1, 1, D_MODEL), lambda i, j: (_mod_row(i), 0, 0 * nb)),
                  pl.BlockSpec((1, 1, D_MODEL), lambda i, j: (_mod_row(i), 0, 1 * nb)),
                  pl.BlockSpec((D_MODEL, TN), lambda i, j: (0, j)),
                  pl.BlockSpec((D_MODEL, LANES), lambda i, j: (0, 0))],
        out_specs=[pl.BlockSpec((TM, TN), lambda i, j: (i, j)),
                   pl.BlockSpec((TM, LANES), lambda i, j: (i, 0))],
        out_shape=[jax.ShapeDtypeStruct((M_TOK, N_MAIN), F32),
                   jax.ShapeDtypeStruct((M_TOK, LANES), F32)],
        scratch_shapes=[pltpu.VMEM((TM, D_MODEL), BF16)],
        compiler_params=_cparams(("arbitrary", "arbitrary")),
        name="proj_in",
    )(x, nw, mod, mod, w_main, w_ab)


def _merge_kernel(r_ref, h_ref, d_ref, pr_ref, ph_ref, pd_ref, gr_ref, gh_ref, gd_ref, o_ref):
    mix = jax.nn.sigmoid(gr_ref[...]) * jnp.dot(r_ref[...], pr_ref[...], preferred_element_type=F32)
    mix += jax.nn.sigmoid(gh_ref[...]) * jnp.dot(h_ref[...], ph_ref[...], preferred_element_type=F32)
    mix += jax.nn.sigmoid(gd_ref[...]) * jnp.dot(d_ref[...], pd_ref[...], preferred_element_type=F32)
    o_ref[...] = mix.astype(BF16)


def _merge(ret_out, hy_out, dn_out, p_ret, p_hy, p_dn, proj):
    g0 = C_MG // TN
    gstep = D_MODEL // TN
    bspec = pl.BlockSpec((TM, RET_W), lambda i, j: (i, 0))
    pspec = pl.BlockSpec((RET_W, TN), lambda i, j: (0, j))
    return pl.pallas_call(
        _merge_kernel,
        grid=(M_TOK // TM, D_MODEL // TN),
        in_specs=[bspec, bspec, bspec, pspec, pspec, pspec,
                  pl.BlockSpec((TM, TN), lambda i, j: (i, g0 + j)),
                  pl.BlockSpec((TM, TN), lambda i, j: (i, g0 + gstep + j)),
                  pl.BlockSpec((TM, TN), lambda i, j: (i, g0 + 2 * gstep + j))],
        out_specs=pl.BlockSpec((TM, TN), lambda i, j: (i, j)),
        out_shape=jax.ShapeDtypeStruct((M_TOK, D_MODEL), BF16),
        compiler_params=_cparams(("arbitrary", "arbitrary")),
        name="branch_merge",
    )(ret_out, hy_out, dn_out, p_ret, p_hy, p_dn, proj, proj, proj)


def _resid_kernel(a_ref, w_ref, x_ref, g_ref, o_ref):
    o_ref[...] = x_ref[...] + g_ref[0] * jnp.dot(a_ref[...], w_ref[...], preferred_element_type=F32)


def _resid_matmul(a, w, x, mod, gate_idx, name):
    k = a.shape[1]
    gb = gate_idx * (D_MODEL // TN)
    return pl.pallas_call(
        _resid_kernel,
        grid=(M_TOK // TM, D_MODEL // TN),
        in_specs=[pl.BlockSpec((TM, k), lambda i, j: (i, 0)),
                  pl.BlockSpec((k, TN), lambda i, j: (0, j)),
                  pl.BlockSpec((TM, TN), lambda i, j: (i, j)),
                  pl.BlockSpec((1, 1, TN), lambda i, j: (_mod_row(i), 0, gb + j))],
        out_specs=pl.BlockSpec((TM, TN), lambda i, j: (i, j)),
        out_shape=jax.ShapeDtypeStruct((M_TOK, D_MODEL), F32),
        compiler_params=_cparams(("arbitrary", "arbitrary")),
        name=name,
    )(a, w, x, mod)


PAD_ROWS = 8


def _ffn_up_kernel(x_ref, nw_ref, sh_ref, sc_ref, wa_ref, wb_ref, ca_ref, cb_ref, o_ref,
                   h_ref, ua_ref, ub_ref):
    i = pl.program_id(0)

    @pl.when(pl.program_id(1) == 0)
    def _():
        h_ref[...] = _modulated_norm(x_ref[...], nw_ref[...], sc_ref[0], sh_ref[0]).astype(BF16)
        zeros = jnp.zeros((PAD_ROWS, TN), F32)
        for ref in (ua_ref, ub_ref):
            ref[pl.ds(0, PAD_ROWS), :] = zeros
            ref[pl.ds(PAD_ROWS + TM, PAD_ROWS), :] = zeros

    h = h_ref[...]
    ua_ref[pl.ds(PAD_ROWS, TM), :] = jnp.dot(h, wa_ref[...], preferred_element_type=F32)
    ub_ref[pl.ds(PAD_ROWS, TM), :] = jnp.dot(h, wb_ref[...], preferred_element_type=F32)

    seq = jnp.where(i < MP // TM, SEQ, DEC_SEQ)
    pos = lax.broadcasted_iota(jnp.int32, (TM, 1), 0) & (seq - 1)
    has_prev = pos != 0
    has_next = pos != seq - 1

    def conv(u_ref, c_ref):
        prev = jnp.where(has_prev, u_ref[pl.ds(PAD_ROWS - 1, TM), :], 0.0)
        nxt = jnp.where(has_next, u_ref[pl.ds(PAD_ROWS + 1, TM), :], 0.0)
        mid = u_ref[pl.ds(PAD_ROWS, TM), :]
        return prev * c_ref[0:1, :] + mid * c_ref[1:2, :] + nxt * c_ref[2:3, :]

    ga = conv(ua_ref, ca_ref)
    gb = conv(ub_ref, cb_ref)
    o_ref[...] = (ga * jax.nn.sigmoid(ga) * gb).astype(BF16)


def _ffn_up(x, nw, mod, w_up, conv_w):
    nj = FFN_PAD // TN
    return pl.pallas_call(
        _ffn_up_kernel,
        grid=(M_TOK // TM, nj),
        in_specs=[pl.BlockSpec((TM, D_MODEL), lambda i, j: (i, 0)),
                  pl.BlockSpec((1, D_MODEL), lambda i, j: (0, 0)),
                  pl.BlockSpec((1, 1, D_MODEL), lambda i, j: (_mod_row(i), 0, 3)),
                  pl.BlockSpec((1, 1, D_MODEL), lambda i, j: (_mod_row(i), 0, 4)),
                  pl.BlockSpec((D_MODEL, TN), lambda i, j: (0, j)),
                  pl.BlockSpec((D_MODEL, TN), lambda i, j: (0, nj + j)),
                  pl.BlockSpec((3, TN), lambda i, j: (0, j)),
                  pl.BlockSpec((3, TN), lambda i, j: (0, nj + j))],
        out_specs=pl.BlockSpec((TM, TN), lambda i, j: (i, j)),
        out_shape=jax.ShapeDtypeStruct((M_TOK, FFN_PAD), BF16),
        scratch_shapes=[pltpu.VMEM((TM, D_MODEL), BF16),
                        pltpu.VMEM((TM + 2 * PAD_ROWS, TN), F32),
                        pltpu.VMEM((TM + 2 * PAD_ROWS, TN), F32)],
        compiler_params=_cparams(("arbitrary", "arbitrary")),
        name="ffn_up",
    )(x, nw, mod, mod, w_up, w_up, conv_w, conv_w)


def _final_norm_kernel(x_ref, w_ref, o_ref):
    x = x_ref[...]
    var = jnp.mean(x * x, axis=-1, keepdims=True)
    o_ref[...] = x * lax.rsqrt(var + EPS) * w_ref[...]


def _final_norm(x, w):
    return pl.pallas_call(
        _final_norm_kernel,
        grid=(M_TOK // TM,),
        in_specs=[pl.BlockSpec((TM, D_MODEL), lambda i: (i, 0)),
                  pl.BlockSpec((1, D_MODEL), lambda i: (0, 0))],
        out_specs=pl.BlockSpec((TM, D_MODEL), lambda i: (i, 0)),
        out_shape=jax.ShapeDtypeStruct((M_TOK, D_MODEL), F32),
        compiler_params=_cparams(("arbitrary",)),
        name="final_norm",
    )(x, w)


def _rms_normalize(x):
    return x * lax.rsqrt(jnp.mean(x * x, axis=-1, keepdims=True) + EPS)


def _l2norm(x):
    return x * lax.rsqrt(jnp.sum(x * x, axis=-1, keepdims=True) + EPS)


def _dwconv(x, w):
    k, ch = w.shape
    return lax.conv_general_dilated(x, w[:, None, :].astype(x.dtype), window_strides=(1,),
                                    padding=[(k // 2, k // 2)],
                                    dimension_numbers=('NWC', 'WIO', 'NWC'),
                                    feature_group_count=ch)


def _rope_2d(x):
    _, L, _, dk = x.shape
    n_rows = L // GRID_W
    pos_r = jnp.repeat(jnp.arange(n_rows, dtype=F32), GRID_W)
    pos_c = jnp.tile(jnp.arange(GRID_W, dtype=F32), n_rows)
    nf = dk // 4
    inv = ROPE_BASE ** (-jnp.arange(nf, dtype=F32) / nf)

    def rot(xh, pos):
        ang = pos[:, None] * inv[None, :]
        cos = jnp.cos(ang)[None, :, None, :]
        sin = jnp.sin(ang)[None, :, None, :]
        x1, x2 = xh[..., :nf], xh[..., nf:]
        return jnp.concatenate([x1 * cos - x2 * sin, x1 * sin + x2 * cos], axis=-1)

    half = dk // 2
    return jnp.concatenate([rot(x[..., :half], pos_r), rot(x[..., half:], pos_c)], axis=-1)


def _retention_scan(q, k, v, log_g, s0):
    B, L, H, _ = q.shape
    dv = v.shape[-1]
    C = RET_CHUNK
    n = L // C

    def blk(t):
        return jnp.moveaxis(t.reshape(B, n, C, H, t.shape[-1]), (1, 3), (0, 2))

    idx = jnp.arange(C, dtype=F32)
    rel = idx[:, None] - idx[None, :]
    causal = rel >= 0
    dmat = jnp.where(causal, jnp.exp(jnp.where(causal, rel, 0.0)[None] * log_g[:, None, None]), 0.0)
    q_dec = jnp.exp((idx + 1.0)[None, :] * log_g[:, None])[:, :, None]
    k_dec = jnp.exp((C - 1.0 - idx)[None, :] * log_g[:, None])[:, :, None]
    c_dec = jnp.exp(C * log_g)[:, None, None]

    def step(s, inp):
        qc, kc, vc = inp
        scores = jnp.einsum('bhid,bhjd->bhij', qc, kc) * dmat
        o = (jnp.einsum('bhij,bhjv->bhiv', scores, vc)
             + jnp.einsum('bhid,bhdv->bhiv', qc * q_dec, s))
        s = s * c_dec + jnp.einsum('bhjd,bhjv->bhdv', kc * k_dec, vc)
        return s, o

    s_fin, o = lax.scan(step, s0, (blk(q), blk(k), blk(v)))
    o = jnp.moveaxis(o, (0, 2), (1, 3)).reshape(B, L, H, dv)
    return o, s_fin


def _gated_delta_scan(q, k, v, g, beta, s0):
    B, L, H, _ = q.shape
    dv = v.shape[-1]
    C = DN_CHUNK
    n = L // C

    def blk(t):
        return jnp.moveaxis(t.reshape((B, n, C, H) + t.shape[3:]), 3, 1)

    q, k, v, g, beta = blk(q), blk(k), blk(v), blk(g), blk(beta)
    gc = jnp.cumsum(g, axis=-1)
    idx = jnp.arange(C)
    incl = idx[:, None] >= idx[None, :]
    strict = idx[:, None] > idx[None, :]
    decay = jnp.exp(jnp.where(incl, gc[..., :, None] - gc[..., None, :], -jnp.inf))
    kb = k * beta[..., None]
    a_mat = (jnp.where(strict, jnp.einsum('bhncd,bhnsd->bhncs', kb, k) * decay, 0.0)
             + jnp.eye(C, dtype=F32))
    rhs = jnp.concatenate([v * beta[..., None], kb * jnp.exp(gc)[..., None]], axis=-1)
    sol = lax.linalg.triangular_solve(a_mat, rhs, left_side=True, lower=True, unit_diagonal=True)
    u, w = sol[..., :dv], sol[..., dv:]
    attn = jnp.einsum('bhncd,bhnsd->bhncs', q, k) * decay
    g_last = gc[..., -1]
    qd = q * jnp.exp(gc)[..., None]
    kd = k * jnp.exp(g_last[..., None] - gc)[..., None]

    def step(s, inp):
        qd_c, kd_c, u_c, w_c, a_c, gl_c = inp
        v_new = u_c - jnp.einsum('bhcd,bhdv->bhcv', w_c, s)
        o = jnp.einsum('bhcd,bhdv->bhcv', qd_c, s) + jnp.einsum('bhcs,bhsv->bhcv', a_c, v_new)
        s = s * jnp.exp(gl_c)[..., None, None] + jnp.einsum('bhcd,bhcv->bhdv', kd_c, v_new)
        return s, o

    xs = (jnp.moveaxis(qd, 2, 0), jnp.moveaxis(kd, 2, 0), jnp.moveaxis(u, 2, 0),
          jnp.moveaxis(w, 2, 0), jnp.moveaxis(attn, 2, 0), jnp.moveaxis(g_last, 2, 0))
    s_fin, o = lax.scan(step, s0, xs)
    o = jnp.moveaxis(o, 0, 2).reshape(B, H, L, dv)
    return jnp.moveaxis(o, 1, 2), s_fin


def _hyena_filters(L, w1, b1, fr1, w2, b2, fr2, w3):
    t = jnp.linspace(0.0, 1.0, L, dtype=F32)[:, None]
    bands = (HY_EMB - 1) // 2
    wpos = 2.0 * math.pi * jnp.arange(L, dtype=F32)[:, None] / L
    fr = jnp.linspace(1e-4, bands - 1, bands, dtype=F32)[None, :]
    feats = jnp.concatenate([t, jnp.cos(fr * wpos), -jnp.sin(fr * wpos)], axis=-1)
    hid = jnp.sin(fr1 * (feats @ w1 + b1))
    hid = jnp.sin(fr2 * (hid @ w2 + b2))
    h = (hid @ w3).reshape(L, 2, HY_ORDER, HY_W)
    deltas = jnp.abs(jnp.linspace(math.log(HY_TARGET) / HY_FAST, math.log(HY_TARGET) / HY_SLOW,
                                  HY_W, dtype=F32))
    h = h * jnp.exp(-t * deltas[None, :])[:, None, None, :]
    return h / (jnp.sum(jnp.abs(h), axis=0, keepdims=True) + EPS)


def _fft_long_conv(z, h_f, h_b, bias):
    L = z.shape[1]
    n = 2 * L
    zs = jnp.fft.rfft(z, n=n, axis=1)
    hs = jnp.fft.rfft(h_f, n=n, axis=0) + jnp.conj(jnp.fft.rfft(h_b, n=n, axis=0))
    y = jnp.fft.irfft(zs * hs[None], n=n, axis=1)[:, :L]
    return y + z * bias


def _mixers_jax(proj, ab, s_ret0, s_dn0, latent, p):
    B, L, _ = proj.shape
    fl = lambda t: jnp.flip(t, axis=1)
    rq = proj[..., C_RQ:C_RK]
    rk = proj[..., C_RK:C_RV]
    rv = proj[..., C_RV:C_RG]
    rg = proj[..., C_RG:C_HY]
    hy = proj[..., C_HY:C_DQKV]
    dqkv = proj[..., C_DQKV:C_DZ]
    dz = proj[..., C_DZ:C_MG]
    da = ab[..., :2 * DN_HEADS]
    db = ab[..., 2 * DN_HEADS:4 * DN_HEADS]

    q = rq.reshape(B, L, RET_HEADS, RET_DK)
    k = rk.reshape(B, L, RET_HEADS, RET_DK) * (RET_DK ** -0.5)
    if latent:
        q, k = _rope_2d(q), _rope_2d(k)
    v = rv.reshape(B, L, RET_HEADS, RET_DV)
    lg = jax.nn.log_sigmoid(p['ret_decay'])
    o_f, sr_f = _retention_scan(q, k, v, lg[0], s_ret0[:, 0])
    o_b, sr_b = _retention_scan(fl(q), fl(k), fl(v), lg[1], s_ret0[:, 1])
    o_r = _rms_normalize(o_f + fl(o_b)).reshape(B, L, RET_W)
    ret_out = o_r * jax.nn.silu(rg)

    filt = _hyena_filters(L, p['hy_w1'], p['hy_b1'], p['hy_freq1'], p['hy_w2'], p['hy_b2'],
                          p['hy_freq2'], p['hy_w3'])
    u = _dwconv(hy, p['hy_short'])
    hv, x1, x2 = jnp.split(u, 3, axis=-1)
    z = x1 * _fft_long_conv(hv, filt[:, 0, 0], filt[:, 1, 0], p['hy_bias'][0])
    hy_out = x2 * _fft_long_conv(z, filt[:, 0, 1], filt[:, 1, 1], p['hy_bias'][1])

    qkv = jax.nn.silu(_dwconv(dqkv, p['dn_conv']))
    q2, k2, v2 = jnp.split(qkv, [DN_QK, 2 * DN_QK], axis=-1)
    q2 = _l2norm(q2.reshape(B, L, DN_HEADS, DN_DK)) * (DN_DK ** -0.5)
    k2 = _l2norm(k2.reshape(B, L, DN_HEADS, DN_DK))
    v2 = v2.reshape(B, L, DN_HEADS, DN_DV)
    beta = jax.nn.sigmoid(db).reshape(B, L, 2, DN_HEADS)
    gdec = (-jnp.exp(p['dn_a_log'])
            * jax.nn.softplus(da.reshape(B, L, 2, DN_HEADS) + p['dn_dt_bias']))
    od_f, sd_f = _gated_delta_scan(q2, k2, v2, gdec[:, :, 0], beta[:, :, 0], s_dn0[:, 0])
    od_b, sd_b = _gated_delta_scan(fl(q2), fl(k2), fl(v2), fl(gdec[:, :, 1]), fl(beta[:, :, 1]),
                                   s_dn0[:, 1])
    o_d = (_rms_normalize(od_f + fl(od_b)) * p['dn_norm']).reshape(B, L, DN_W)
    dn_out = o_d * jax.nn.silu(dz)
    return (ret_out, hy_out, dn_out,
            jnp.stack([sr_f, sr_b], axis=1), jnp.stack([sd_f, sd_b], axis=1))


def _prep_weights(w_in, p_ret, p_hy, p_dn, w_o, w_up, ffn_conv, w_down):
    w_main = jnp.concatenate([w_in[:, :O_DA], w_in[:, O_MG:]], axis=1).astype(BF16)
    w_ab = jnp.pad(w_in[:, O_DA:O_MG], ((0, 0), (0, LANES - (O_MG - O_DA)))).astype(BF16)
    padc = FFN_PAD - FFN_DIM
    wu = jnp.concatenate([jnp.pad(w_up[:, :FFN_DIM], ((0, 0), (0, padc))),
                          jnp.pad(w_up[:, FFN_DIM:], ((0, 0), (0, padc)))], axis=1).astype(BF16)
    cw = jnp.concatenate([jnp.pad(ffn_conv[:, :FFN_DIM], ((0, 0), (0, padc))),
                          jnp.pad(ffn_conv[:, FFN_DIM:], ((0, 0), (0, padc)))], axis=1)
    wd = jnp.pad(w_down, ((0, padc), (0, 0))).astype(BF16)
    return (w_main, w_ab, p_ret.astype(BF16), p_hy.astype(BF16), p_dn.astype(BF16),
            w_o.astype(BF16), wu, cw, wd)


def kernel(x_prompt, x_sample, state_ret, state_dn, c, c_ctx, w_ada, b_ada, norm1, w_in, ret_decay, hy_short, hy_w1, hy_b1, hy_freq1, hy_w2, hy_b2, hy_freq2, hy_w3, hy_bias, dn_conv, dn_a_log, dn_dt_bias, dn_norm, p_ret, p_hy, p_dn, w_o, norm2, w_up, ffn_conv, w_down, norm_f):
    x = jnp.concatenate([x_prompt.reshape(MP, D_MODEL), x_sample.reshape(MS, D_MODEL)], axis=0)
    cond = jnp.concatenate([c_ctx[None], c, jnp.zeros((3, D_MODEL), F32)], axis=0)
    mod_all = _adaln(cond, w_ada, b_ada)
    zero_ret = jnp.zeros((BATCH, 2, RET_HEADS, RET_DK, RET_DV), F32)
    zero_dn = jnp.zeros((BATCH, 2, DN_HEADS, DN_DK, DN_DV), F32)
    new_ret, new_dn = [], []
    for l in range(DEPTH):
        p = dict(ret_decay=ret_decay[l], hy_short=hy_short[l], hy_w1=hy_w1[l], hy_b1=hy_b1[l],
                 hy_freq1=hy_freq1[l], hy_w2=hy_w2[l], hy_b2=hy_b2[l], hy_freq2=hy_freq2[l],
                 hy_w3=hy_w3[l], hy_bias=hy_bias[l], dn_conv=dn_conv[l], dn_a_log=dn_a_log[l],
                 dn_dt_bias=dn_dt_bias[l], dn_norm=dn_norm[l])
        (w_main, w_ab, pr, ph, pd, wo, wu, cw, wd) = _prep_weights(
            w_in[l], p_ret[l], p_hy[l], p_dn[l], w_o[l], w_up[l], ffn_conv[l], w_down[l])
        mod = mod_all[l].reshape(8, 1, N_MOD * D_MODEL)
        proj, ab = _proj_in(x, norm1[l].reshape(1, D_MODEL), mod, w_main, w_ab)

        rp, hp, dp, s_r, s_d = _mixers_jax(
            proj[:MP].reshape(BATCH, SEQ, N_MAIN), ab[:MP].reshape(BATCH, SEQ, LANES),
            zero_ret, zero_dn, False, p)
        rs, hs, ds, _, _ = _mixers_jax(
            proj[MP:].reshape(DEC_BATCH, DEC_SEQ, N_MAIN), ab[MP:].reshape(DEC_BATCH, DEC_SEQ, LANES),
            state_ret[:, l], state_dn[:, l], True, p)
        new_ret.append(s_r)
        new_dn.append(s_d)
        cat = lambda a, b: jnp.concatenate(
            [a.reshape(MP, -1), b.reshape(MS, -1)], axis=0).astype(BF16)
        mix = _merge(cat(rp, rs), cat(hp, hs), cat(dp, ds), pr, ph, pd, proj)
        x = _resid_matmul(mix, wo, x, mod, 2, "attn_out")
        act = _ffn_up(x, norm2[l].reshape(1, D_MODEL), mod, wu, cw)
        x = _resid_matmul(act, wd, x, mod, 5, "ffn_down")
    y = _final_norm(x, norm_f.reshape(1, D_MODEL))
    y_prompt = y[:MP].reshape(BATCH, SEQ, D_MODEL)
    y_sample = y[MP:].reshape(DEC_BATCH, DEC_SEQ, D_MODEL)
    return (y_prompt, y_sample, jnp.stack(new_ret, axis=1), jnp.stack(new_dn, axis=1))
```

```python
import functools
import math

import jax
import jax.numpy as jnp
import numpy as np
from jax import lax
from jax.experimental import pallas as pl
from jax.experimental.pallas import tpu as pltpu

F32 = jnp.float32
BF16 = jnp.bfloat16

D_MODEL = 2048
BATCH = 32
SEQ = 256
DEPTH = 2
DEC_BATCH = 4
DEC_SEQ = 1024
GRID_W = 64
EPS = 1e-6
ROPE_BASE = 10000.0
RET_HEADS = 8
RET_DK = 64
RET_DV = 128
RET_QK = RET_HEADS * RET_DK
RET_W = RET_HEADS * RET_DV
RET_CHUNK = 128
HY_W = 1024
HY_ORDER = 2
HY_EMB = 33
HY_HID = 64
HY_TARGET = 1e-2
HY_FAST = 0.3
HY_SLOW = 1.5
DN_HEADS = 8
DN_DK = 128
DN_DV = 128
DN_QK = DN_HEADS * DN_DK
DN_W = DN_HEADS * DN_DV
DN_CHUNK = 64
N_BRANCH = 3
FFN_DIM = 5504
N_MOD = 6

MP = BATCH * SEQ
MS = DEC_BATCH * DEC_SEQ
M_TOK = MP + MS

LANES = 128
TM = 1024
TN = 512
FFN_PAD = 5632
HY_TC_P = 1024
HY_TC_S = 256
VMEM_LIMIT = 56 * 1024 * 1024

C_RQ, C_RK, C_RV, C_RG = 0, 512, 1024, 2048
C_HY, C_DQKV, C_DZ, C_MG = 3072, 6144, 9216, 10240
N_MAIN = 16384
O_DA, O_MG = 10240, 10272


def _cparams(sem):
    return pltpu.CompilerParams(dimension_semantics=sem, vmem_limit_bytes=VMEM_LIMIT)


def _mod_row(i):
    return jnp.where(i < MP // TM, 0, 1 + (i - MP // TM) // (DEC_SEQ // TM))


def _mod_kernel(cond_ref, w_ref, b_ref, o_ref):
    c = cond_ref[...]
    s = (c * jax.nn.sigmoid(c)).astype(BF16)
    o_ref[0] = jnp.dot(s, w_ref[0].astype(BF16), preferred_element_type=F32) + b_ref[0]


def _adaln(cond, w_ada, b_ada):
    tn = 1024
    return pl.pallas_call(
        _mod_kernel,
        grid=(DEPTH, N_MOD * D_MODEL // tn),
        in_specs=[pl.BlockSpec((8, D_MODEL), lambda l, j: (0, 0)),
                  pl.BlockSpec((1, D_MODEL, tn), lambda l, j: (l, 0, j)),
                  pl.BlockSpec((1, 1, tn), lambda l, j: (l, 0, j))],
        out_specs=pl.BlockSpec((1, 8, tn), lambda l, j: (l, 0, j)),
        out_shape=jax.ShapeDtypeStruct((DEPTH, 8, N_MOD * D_MODEL), F32),
        compiler_params=_cparams(("arbitrary", "arbitrary")),
        name="adaln_mod",
    )(cond, w_ada, b_ada.reshape(DEPTH, 1, N_MOD * D_MODEL))


def _modulated_norm(x, nw, sc, sh):
    var = jnp.mean(x * x, axis=-1, keepdims=True)
    return (x * lax.rsqrt(var + EPS) * nw) * (1.0 + sc) + sh


def _proj_in_kernel(x_ref, nw_ref, sh_ref, sc_ref, w_ref, wab_ref, o_ref, ab_ref, h_ref):
    @pl.when(pl.program_id(1) == 0)
    def _():
        h = _modulated_norm(x_ref[...], nw_ref[...], sc_ref[0], sh_ref[0]).astype(BF16)
        h_ref[...] = h
        ab_ref[...] = jnp.dot(h, wab_ref[...], preferred_element_type=F32)

    o_ref[...] = jnp.dot(h_ref[...], w_ref[...], preferred_element_type=F32)


def _proj_in(x, nw, mod, w_main, w_ab):
    nb = D_MODEL // D_MODEL
    return pl.pallas_call(
        _proj_in_kernel,
        grid=(M_TOK // TM, N_MAIN // TN),
        in_specs=[pl.BlockSpec((TM, D_MODEL), lambda i, j: (i, 0)),
                  pl.BlockSpec((1, D_MODEL), lambda i, j: (0, 0)),
                  pl.BlockSpec((1, 1, D_MODEL), lambda i, j: (_mod_row(i), 0, 0 * nb)),
                  pl.BlockSpec((1, 1, D_MODEL), lambda i, j: (_mod_row(i), 0, 1 * nb)),
                  pl.BlockSpec((D_MODEL, TN), lambda i, j: (0, j)),
                  pl.BlockSpec((D_MODEL, LANES), lambda i, j: (0, 0))],
        out_specs=[pl.BlockSpec((TM, TN), lambda i, j: (i, j)),
                   pl.BlockSpec((TM, LANES), lambda i, j: (i, 0))],
        out_shape=[jax.ShapeDtypeStruct((M_TOK, N_MAIN), F32),
                   jax.ShapeDtypeStruct((M_TOK, LANES), F32)],
        scratch_shapes=[pltpu.VMEM((TM, D_MODEL), BF16)],
        compiler_params=_cparams(("arbitrary", "arbitrary")),
        name="proj_in",
    )(x, nw, mod, mod, w_main, w_ab)


def _merge_kernel(r_ref, h_ref, d_ref, pr_ref, ph_ref, pd_ref, gr_ref, gh_ref, gd_ref, o_ref):
    mix = jax.nn.sigmoid(gr_ref[...]) * jnp.dot(r_ref[...], pr_ref[...], preferred_element_type=F32)
    mix += jax.nn.sigmoid(gh_ref[...]) * jnp.dot(h_ref[...], ph_ref[...], preferred_element_type=F32)
    mix += jax.nn.sigmoid(gd_ref[...]) * jnp.dot(d_ref[...], pd_ref[...], preferred_element_type=F32)
    o_ref[...] = mix.astype(BF16)


def _merge(ret_out, hy_out, dn_out, p_ret, p_hy, p_dn, proj):
    g0 = C_MG // TN
    gstep = D_MODEL // TN
    bspec = pl.BlockSpec((TM, RET_W), lambda i, j: (i, 0))
    pspec = pl.BlockSpec((RET_W, TN), lambda i, j: (0, j))
    return pl.pallas_call(
        _merge_kernel,
        grid=(M_TOK // TM, D_MODEL // TN),
        in_specs=[bspec, bspec, bspec, pspec, pspec, pspec,
                  pl.BlockSpec((TM, TN), lambda i, j: (i, g0 + j)),
                  pl.BlockSpec((TM, TN), lambda i, j: (i, g0 + gstep + j)),
                  pl.BlockSpec((TM, TN), lambda i, j: (i, g0 + 2 * gstep + j))],
        out_specs=pl.BlockSpec((TM, TN), lambda i, j: (i, j)),
        out_shape=jax.ShapeDtypeStruct((M_TOK, D_MODEL), BF16),
        compiler_params=_cparams(("arbitrary", "arbitrary")),
        name="branch_merge",
    )(ret_out, hy_out, dn_out, p_ret, p_hy, p_dn, proj, proj, proj)


def _resid_kernel(a_ref, w_ref, x_ref, g_ref, o_ref):
    o_ref[...] = x_ref[...] + g_ref[0] * jnp.dot(a_ref[...], w_ref[...], preferred_element_type=F32)


def _resid_matmul(a, w, x, mod, gate_idx, name):
    k = a.shape[1]
    gb = gate_idx * (D_MODEL // TN)
    return pl.pallas_call(
        _resid_kernel,
        grid=(M_TOK // TM, D_MODEL // TN),
        in_specs=[pl.BlockSpec((TM, k), lambda i, j: (i, 0)),
                  pl.BlockSpec((k, TN), lambda i, j: (0, j)),
                  pl.BlockSpec((TM, TN), lambda i, j: (i, j)),
                  pl.BlockSpec((1, 1, TN), lambda i, j: (_mod_row(i), 0, gb + j))],
        out_specs=pl.BlockSpec((TM, TN), lambda i, j: (i, j)),
        out_shape=jax.ShapeDtypeStruct((M_TOK, D_MODEL), F32),
        compiler_params=_cparams(("arbitrary", "arbitrary")),
        name=name,
    )(a, w, x, mod)


PAD_ROWS = 8


def _ffn_up_kernel(x_ref, nw_ref, sh_ref, sc_ref, wa_ref, wb_ref, ca_ref, cb_ref, o_ref,
                   h_ref, ua_ref, ub_ref):
    i = pl.program_id(0)

    @pl.when(pl.program_id(1) == 0)
    def _():
        h_ref[...] = _modulated_norm(x_ref[...], nw_ref[...], sc_ref[0], sh_ref[0]).astype(BF16)
        zeros = jnp.zeros((PAD_ROWS, TN), F32)
        for ref in (ua_ref, ub_ref):
            ref[pl.ds(0, PAD_ROWS), :] = zeros
            ref[pl.ds(PAD_ROWS + TM, PAD_ROWS), :] = zeros

    h = h_ref[...]
    ua_ref[pl.ds(PAD_ROWS, TM), :] = jnp.dot(h, wa_ref[...], preferred_element_type=F32)
    ub_ref[pl.ds(PAD_ROWS, TM), :] = jnp.dot(h, wb_ref[...], preferred_element_type=F32)

    seq = jnp.where(i < MP // TM, SEQ, DEC_SEQ)
    pos = lax.broadcasted_iota(jnp.int32, (TM, 1), 0) & (seq - 1)
    has_prev = pos != 0
    has_next = pos != seq - 1

    def conv(u_ref, c_ref):
        prev = jnp.where(has_prev, u_ref[pl.ds(PAD_ROWS - 1, TM), :], 0.0)
        nxt = jnp.where(has_next, u_ref[pl.ds(PAD_ROWS + 1, TM), :], 0.0)
        mid = u_ref[pl.ds(PAD_ROWS, TM), :]
        return prev * c_ref[0:1, :] + mid * c_ref[1:2, :] + nxt * c_ref[2:3, :]

    ga = conv(ua_ref, ca_ref)
    gb = conv(ub_ref, cb_ref)
    o_ref[...] = (ga * jax.nn.sigmoid(ga) * gb).astype(BF16)


def _ffn_up(x, nw, mod, w_up, conv_w):
    nj = FFN_PAD // TN
    return pl.pallas_call(
        _ffn_up_kernel,
        grid=(M_TOK // TM, nj),
        in_specs=[pl.BlockSpec((TM, D_MODEL), lambda i, j: (i, 0)),
                  pl.BlockSpec((1, D_MODEL), lambda i, j: (0, 0)),
                  pl.BlockSpec((1, 1, D_MODEL), lambda i, j: (_mod_row(i), 0, 3)),
                  pl.BlockSpec((1, 1, D_MODEL), lambda i, j: (_mod_row(i), 0, 4)),
                  pl.BlockSpec((D_MODEL, TN), lambda i, j: (0, j)),
                  pl.BlockSpec((D_MODEL, TN), lambda i, j: (0, nj + j)),
                  pl.BlockSpec((3, TN), lambda i, j: (0, j)),
                  pl.BlockSpec((3, TN), lambda i, j: (0, nj + j))],
        out_specs=pl.BlockSpec((TM, TN), lambda i, j: (i, j)),
        out_shape=jax.ShapeDtypeStruct((M_TOK, FFN_PAD), BF16),
        scratch_shapes=[pltpu.VMEM((TM, D_MODEL), BF16),
                        pltpu.VMEM((TM + 2 * PAD_ROWS, TN), F32),
                        pltpu.VMEM((TM + 2 * PAD_ROWS, TN), F32)],
        compiler_params=_cparams(("arbitrary", "arbitrary")),
        name="ffn_up",
    )(x, nw, mod, mod, w_up, w_up, conv_w, conv_w)


def _final_norm_kernel(x_ref, w_ref, o_ref):
    x = x_ref[...]
    var = jnp.mean(x * x, axis=-1, keepdims=True)
    o_ref[...] = x * lax.rsqrt(var + EPS) * w_ref[...]


def _final_norm(x, w):
    return pl.pallas_call(
        _final_norm_kernel,
        grid=(M_TOK // TM,),
        in_specs=[pl.BlockSpec((TM, D_MODEL), lambda i: (i, 0)),
                  pl.BlockSpec((1, D_MODEL), lambda i: (0, 0))],
        out_specs=pl.BlockSpec((TM, D_MODEL), lambda i: (i, 0)),
        out_shape=jax.ShapeDtypeStruct((M_TOK, D_MODEL), F32),
        compiler_params=_cparams(("arbitrary",)),
        name="final_norm",
    )(x, w)


def _dft_mats(L):
    n = 2 * L
    t = np.arange(L)
    ang = 2.0 * np.pi * ((t[:, None] * t[None, :]) % n) / n
    fc, fs = np.cos(ang), np.sin(ang)
    alt = (-1.0) ** t
    fs[0, :] = alt
    fwd = np.concatenate([fc, fs], axis=0)
    w = np.full((L, 1), 2.0)
    w[0] = 1.0
    gc = (fc * w / n).T
    gs = (fs * 2.0 / n).T
    gs[:, 0] = alt / n
    inv = np.concatenate([gc, gs], axis=1)
    return jnp.asarray(fwd, BF16), jnp.asarray(inv, BF16)


def _hy_feats(L):
    t = jnp.linspace(0.0, 1.0, L, dtype=F32)[:, None]
    bands = (HY_EMB - 1) // 2
    wpos = 2.0 * math.pi * jnp.arange(L, dtype=F32)[:, None] / L
    fr = jnp.linspace(1e-4, bands - 1, bands, dtype=F32)[None, :]
    feats = jnp.concatenate([t, jnp.cos(fr * wpos), -jnp.sin(fr * wpos)], axis=-1)
    return jnp.pad(feats, ((0, 0), (0, LANES - HY_EMB))), t


def _hy_filter_kernel(feats_ref, t_ref, w1_ref, b1_ref, f1_ref, w2_ref, b2_ref, f2_ref,
                      w3f_ref, w3b_ref, delta_ref, fwd_ref, hc_ref, hs_ref, *, L):
    hid = jnp.dot(feats_ref[...].astype(BF16), w1_ref[...].astype(BF16), preferred_element_type=F32)
    hid = jnp.sin(f1_ref[...] * (hid + b1_ref[...]))
    hid = jnp.dot(hid.astype(BF16), w2_ref[...].astype(BF16), preferred_element_type=F32)
    hid = jnp.sin(f2_ref[...] * (hid + b2_ref[...])).astype(BF16)
    win = jnp.exp(-t_ref[...] * delta_ref[...])

    def spectrum(w3_ref):
        h = jnp.dot(hid, w3_ref[...].astype(BF16), preferred_element_type=F32) * win
        h = h / (jnp.sum(jnp.abs(h), axis=0, keepdims=True) + EPS)
        return jnp.dot(fwd_ref[...], h.astype(BF16), preferred_element_type=F32)

    a = spectrum(w3f_ref)
    b = spectrum(w3b_ref)
    dc = lax.broadcasted_iota(jnp.int32, (L, 1), 0) == 0
    hc_ref[0] = a[:L] + b[:L]
    hs_ref[0] = a[L:] + jnp.where(dc, b[L:], -b[L:])


def _hy_filter(L, tc, fwd, w1, b1, f1, w2, b2, f2, w3):
    feats, t = _hy_feats(L)
    deltas = jnp.abs(jnp.linspace(math.log(HY_TARGET) / HY_FAST, math.log(HY_TARGET) / HY_SLOW,
                                  HY_W, dtype=F32))[None, :]
    w1p = jnp.pad(w1, ((0, LANES - HY_EMB), (0, 0)))
    nj = HY_W // tc
    full = lambda shape: pl.BlockSpec(shape, lambda o, j: (0,) * len(shape))
    out = jax.ShapeDtypeStruct((HY_ORDER, L, HY_W), F32)
    return pl.pallas_call(
        functools.partial(_hy_filter_kernel, L=L),
        grid=(HY_ORDER, nj),
        in_specs=[full((L, LANES)), full((L, 1)), full((LANES, HY_HID)), full((1, HY_HID)),
                  full((1, HY_HID)), full((HY_HID, HY_HID)), full((1, HY_HID)), full((1, HY_HID)),
                  pl.BlockSpec((HY_HID, tc), lambda o, j: (0, o * nj + j)),
                  pl.BlockSpec((HY_HID, tc), lambda o, j: (0, (HY_ORDER + o) * nj + j)),
                  pl.BlockSpec((1, tc), lambda o, j: (0, j)),
                  full((2 * L, L))],
        out_specs=[pl.BlockSpec((1, L, tc), lambda o, j: (o, 0, j)),
                   pl.BlockSpec((1, L, tc), lambda o, j: (o, 0, j))],
        out_shape=[out, out],
        compiler_params=_cparams(("arbitrary", "arbitrary")),
        name=f"hyena_filter_{L}",
    )(feats, t, w1p, b1.reshape(1, HY_HID), f1.reshape(1, HY_HID), w2, b2.reshape(1, HY_HID),
      f2.reshape(1, HY_HID), w3, w3, deltas, fwd)


def _hy_conv_kernel(v_ref, x1_ref, x2_ref, sv_ref, s1_ref, s2_ref, bias_ref, hc_ref, hs_ref,
                    fwd_ref, inv_ref, o_ref, *, L):
    row = lax.broadcasted_iota(jnp.int32, (L, 1), 0)
    first = row == 0
    last = row == L - 1

    def short_conv(u_ref, s_ref):
        u = u_ref[...]
        prev = jnp.where(first, 0.0, pltpu.roll(u, 1, 0))
        nxt = jnp.where(last, 0.0, pltpu.roll(u, L - 1, 0))
        return prev * s_ref[0:1, :] + u * s_ref[1:2, :] + nxt * s_ref[2:3, :]

    def long_conv(z, order):
        spec = jnp.dot(fwd_ref[...], z.astype(BF16), preferred_element_type=F32)
        zc, zs = spec[:L], spec[L:]
        hc, hs = hc_ref[order], hs_ref[order]
        yc = zc * hc - jnp.where(first, 0.0, zs * hs)
        ys = jnp.where(first, zs * hs, zc * hs + zs * hc)
        y = jnp.dot(inv_ref[...], jnp.concatenate([yc, ys], axis=0).astype(BF16),
                    preferred_element_type=F32)
        return y + z * bias_ref[order:order + 1, :]

    z = short_conv(x1_ref, s1_ref) * long_conv(short_conv(v_ref, sv_ref), 0)
    o_ref[...] = (short_conv(x2_ref, s2_ref) * long_conv(z, 1)).astype(BF16)


def _hy_conv(proj, row0, nseq, L, tc, hy_short, hy_bias, hc, hs, fwd, inv):
    nj = HY_W // tc
    rb = row0 // L
    cb = C_HY // tc
    col = lambda k: pl.BlockSpec((L, tc), lambda b, j: (rb + b, cb + k * nj + j))
    sw = lambda k: pl.BlockSpec((3, tc), lambda b, j: (0, k * nj + j))
    spec3 = pl.BlockSpec((HY_ORDER, L, tc), lambda b, j: (0, 0, j))
    return pl.pallas_call(
        functools.partial(_hy_conv_kernel, L=L),
        grid=(nseq, nj),
        in_specs=[col(0), col(1), col(2), sw(0), sw(1), sw(2),
                  pl.BlockSpec((HY_ORDER, tc), lambda b, j: (0, j)), spec3, spec3,
                  pl.BlockSpec((2 * L, L), lambda b, j: (0, 0)),
                  pl.BlockSpec((L, 2 * L), lambda b, j: (0, 0))],
        out_specs=pl.BlockSpec((L, tc), lambda b, j: (b, j)),
        out_shape=jax.ShapeDtypeStruct((nseq * L, HY_W), BF16),
        compiler_params=_cparams(("arbitrary", "arbitrary")),
        name=f"hyena_conv_{L}",
    )(proj, proj, proj, hy_short, hy_short, hy_short, hy_bias, hc, hs, fwd, inv)


def _rope_tables(L):
    n_rows = L // GRID_W
    pos_r = jnp.repeat(jnp.arange(n_rows, dtype=F32), GRID_W)
    pos_c = jnp.tile(jnp.arange(GRID_W, dtype=F32), n_rows)
    nf = RET_DK // 4
    inv = ROPE_BASE ** (-jnp.arange(nf, dtype=F32) / nf)
    ar = pos_r[:, None] * inv[None, :]
    ac = pos_c[:, None] * inv[None, :]
    cos = jnp.concatenate([jnp.cos(ar), jnp.cos(ar), jnp.cos(ac), jnp.cos(ac)], axis=-1)
    sin = jnp.concatenate([-jnp.sin(ar), jnp.sin(ar), -jnp.sin(ac), jnp.sin(ac)], axis=-1)
    return jnp.tile(cos, (1, 2)), jnp.tile(sin, (1, 2))


def _ret_kernel(*refs, L, latent):
    if latent:
        (q_ref, k_ref, v_ref, g_ref, dec_ref, cos_ref, sin_ref, s0_ref, o_ref,
         acc_ref, qs_ref, ks_ref) = refs
    else:
        q_ref, k_ref, v_ref, g_ref, dec_ref, o_ref, sn_ref, acc_ref, qs_ref, ks_ref = refs
    C = RET_CHUNK
    n = L // C
    lane = lax.broadcasted_iota(jnp.int32, (1, 2 * RET_DK), 1)
    q = q_ref[...]
    k = k_ref[...] * (RET_DK ** -0.5)
    if latent:
        nf = RET_DK // 4
        lo = (lane % (2 * nf)) < nf

        def rope(x):
            swapped = jnp.where(lo, pltpu.roll(x, 2 * RET_DK - nf, 1), pltpu.roll(x, nf, 1))
            return x * cos_ref[...] + swapped * sin_ref[...]

        q, k = rope(q), rope(k)
    qs_ref[...] = q
    ks_ref[...] = k
    ri = lax.broadcasted_iota(jnp.int32, (C, C), 0)
    ci = lax.broadcasted_iota(jnp.int32, (C, C), 1)
    rel = (ri - ci).astype(F32)
    rowf = ri.astype(F32)

    for h in range(2):
        head = (lane // RET_DK) == h
        vcol = slice(h * RET_DV, (h + 1) * RET_DV)
        for d in range(2):
            dec = dec_ref[d, 2 * pl.program_id(1) + h]
            lg = jnp.minimum(dec, 0.0) - jnp.log1p(jnp.exp(-jnp.abs(dec)))
            if d == 0:
                dmat = jnp.where(rel >= 0, jnp.exp(rel * lg), 0.0)
                q_dec = jnp.exp((rowf + 1.0) * lg)
                k_dec = jnp.exp((C - 1.0 - rowf) * lg)
            else:
                dmat = jnp.where(rel <= 0, jnp.exp(-rel * lg), 0.0)
                q_dec = jnp.exp((C - rowf) * lg)
                k_dec = jnp.exp(rowf * lg)
            c_dec = jnp.exp(C * lg)
            if latent:
                s_init = s0_ref[0, d, h]
                zero = jnp.zeros_like(s_init)
                s_init = jnp.concatenate([s_init, zero] if h == 0 else [zero, s_init], axis=0)
            else:
                s_init = jnp.zeros((2 * RET_DK, RET_DV), F32)

            s = s_init
            for i in range(n):
                c = i if d == 0 else n - 1 - i
                rows = slice(c * C, (c + 1) * C)
                qc = jnp.where(head, qs_ref[rows, :], 0.0)
                kc = ks_ref[rows, :]
                vc = v_ref[rows, vcol].astype(BF16)
                scores = lax.dot_general(qc.astype(BF16), kc.astype(BF16), (((1,), (1,)), ((), ())),
                                         preferred_element_type=F32) * dmat
                o = jnp.dot(scores.astype(BF16), vc, preferred_element_type=F32)
                o += jnp.dot((qc * q_dec).astype(BF16), s.astype(BF16), preferred_element_type=F32)
                if d == 0:
                    acc_ref[rows, vcol] = o
                else:
                    acc_ref[rows, vcol] += o
                kv = lax.dot_general((kc * k_dec).astype(BF16), vc, (((0,), (0,)), ((), ())),
                                     preferred_element_type=F32)
                s = s * c_dec + kv
            if not latent:
                sn_ref[0, d, h] = s[h * RET_DK:(h + 1) * RET_DK]

    for h in range(2):
        vcol = slice(h * RET_DV, (h + 1) * RET_DV)
        o = acc_ref[:, vcol]
        o = o * lax.rsqrt(jnp.mean(o * o, axis=-1, keepdims=True) + EPS)
        g = g_ref[:, vcol]
        o_ref[:, vcol] = (o * (g * jax.nn.sigmoid(g))).astype(BF16)


def _retention(proj, row0, nseq, L, latent, dec, s0=None):
    rb = row0 // L
    npair = RET_HEADS // 2
    qk = lambda c0: pl.BlockSpec((L, 2 * RET_DK), lambda b, p: (rb + b, c0 // (2 * RET_DK) + p))
    vg = lambda c0: pl.BlockSpec((L, 2 * RET_DV), lambda b, p: (rb + b, c0 // (2 * RET_DV) + p))
    in_specs = [qk(C_RQ), qk(C_RK), vg(C_RV), vg(C_RG),
                pl.BlockSpec((2, RET_HEADS, 1, LANES), lambda b, p: (0, 0, 0, 0))]
    args = [proj, proj, proj, proj, dec]
    out_specs = [pl.BlockSpec((L, 2 * RET_DV), lambda b, p: (b, p))]
    out_shape = [jax.ShapeDtypeStruct((nseq * L, RET_W), BF16)]
    st_spec = pl.BlockSpec((1, 2, 2, RET_DK, RET_DV), lambda b, p: (b, 0, p, 0, 0))
    if latent:
        cos, sin = _rope_tables(L)
        tab = pl.BlockSpec((L, 2 * RET_DK), lambda b, p: (0, 0))
        in_specs += [tab, tab, st_spec]
        args += [cos, sin, s0]
    else:
        out_specs.append(st_spec)
        out_shape.append(jax.ShapeDtypeStruct((nseq, 2, RET_HEADS, RET_DK, RET_DV), F32))
    return pl.pallas_call(
        functools.partial(_ret_kernel, L=L, latent=latent),
        grid=(nseq, npair),
        in_specs=in_specs, out_specs=out_specs, out_shape=out_shape,
        scratch_shapes=[pltpu.VMEM((L, 2 * RET_DV), F32),
                        pltpu.VMEM((L, 2 * RET_DK), F32),
                        pltpu.VMEM((L, 2 * RET_DK), F32)],
        compiler_params=_cparams(("arbitrary", "arbitrary")),
        name=f"retention_{L}",
    )(*args)


DN_C = 128


def _dn_gate_params(a_log, dt_bias):
    gp = jnp.stack([a_log, dt_bias], axis=0)
    return jnp.broadcast_to(gp[:, :, :, None, None], (2, 2, DN_HEADS, 1, LANES))


def _dot16(a, b):
    return jnp.dot(a.astype(BF16), b.astype(BF16), preferred_element_type=F32)


def _dot16_nt(a, b):
    return lax.dot_general(a.astype(BF16), b.astype(BF16), (((1,), (1,)), ((), ())),
                           preferred_element_type=F32)


def _dot16_tn(a, b):
    return lax.dot_general(a.astype(BF16), b.astype(BF16), (((0,), (0,)), ((), ())),
                           preferred_element_type=F32)


def _dot32(a, b):
    return jnp.dot(a, b, preferred_element_type=F32, precision=lax.Precision.HIGHEST)


def _dn_kernel(*refs, L, latent):
    if latent:
        (q_ref, k_ref, v_ref, z_ref, ab_ref, cq_ref, ck_ref, cv_ref, gp_ref, nw_ref, s0_ref,
         o_ref, acc_ref, qs_ref, ks_ref, vs_ref) = refs
    else:
        (q_ref, k_ref, v_ref, z_ref, ab_ref, cq_ref, ck_ref, cv_ref, gp_ref, nw_ref,
         o_ref, sn_ref, acc_ref, qs_ref, ks_ref, vs_ref) = refs
    C = DN_C
    n = L // C
    h = pl.program_id(1)
    row = lax.broadcasted_iota(jnp.int32, (L, 1), 0)
    first = row == 0
    last = row == L - 1

    def conv_silu(u_ref, c_ref):
        u = u_ref[...]
        prev = jnp.where(first, 0.0, pltpu.roll(u, 1, 0))
        nxt = jnp.where(last, 0.0, pltpu.roll(u, L - 1, 0))
        y = prev * c_ref[0:1, :] + u * c_ref[1:2, :] + nxt * c_ref[2:3, :]
        return y * jax.nn.sigmoid(y)

    def l2n(x):
        return x * lax.rsqrt(jnp.sum(x * x, axis=-1, keepdims=True) + EPS)

    qs_ref[...] = l2n(conv_silu(q_ref, cq_ref)) * (DN_DK ** -0.5)
    ks_ref[...] = l2n(conv_silu(k_ref, ck_ref))
    vs_ref[...] = conv_silu(v_ref, cv_ref)

    ri = lax.broadcasted_iota(jnp.int32, (C, C), 0)
    ci = lax.broadcasted_iota(jnp.int32, (C, C), 1)
    rx = ri ^ ci
    eye = (ri == ci).astype(F32)

    for d in range(2):
        lower = d == 0
        incl = (ri >= ci) if lower else (ri <= ci)
        strict = (ri > ci) if lower else (ri < ci)
        tri = incl.astype(F32)
        sel_a = (ri == d * DN_HEADS + h).astype(F32)
        sel_b = (ri == (2 + d) * DN_HEADS + h).astype(F32)
        a_neg = -jnp.exp(gp_ref[0, d, h])
        dt_bias = gp_ref[1, d, h]
        half_bit = ri if lower else ci

        def quadrant(b):
            return (rx >= b // 2) & (rx < b) & ((half_bit & (b // 2)) != 0)

        def chunk(i, s, lower=lower, incl=incl, strict=strict, tri=tri, sel_a=sel_a, sel_b=sel_b,
                  a_neg=a_neg, dt_bias=dt_bias, quadrant=quadrant, d=d):
            c = i if lower else n - 1 - i
            rows = pl.ds(pl.multiple_of(c * C, C), C)
            qc, kc, vc = qs_ref[rows, :], ks_ref[rows, :], vs_ref[rows, :]
            abc = ab_ref[rows, :]
            beta = jax.nn.sigmoid(_dot32(abc, sel_b))
            x = _dot32(abc, sel_a) + dt_bias
            g = a_neg * (jnp.maximum(x, 0.0) + jnp.log1p(jnp.exp(-jnp.abs(x))))
            gc = _dot32(tri, g)
            diff = jnp.where(incl, gc - gc.T, 0.0)
            decay = jnp.where(incl, jnp.exp(diff), 0.0)
            kb = kc * beta
            a_mat = jnp.where(strict, _dot16_nt(kb, kc) * decay, 0.0)
            inv = eye - jnp.where(quadrant(2), a_mat, 0.0)
            b = 4
            while b <= C:
                off = jnp.where(quadrant(b), a_mat, 0.0)
                inv = inv - _dot16(_dot16(inv, off), inv)
                b *= 2
            egc = jnp.exp(gc)
            sol = _dot16(inv, jnp.concatenate([vc * beta, kb * egc], axis=1))
            u, w = sol[:, :DN_DV], sol[:, DN_DV:]
            attn = _dot16_nt(qc, kc) * decay
            g_last = gc[C - 1:C, :] if lower else gc[0:1, :]
            v_new = u - _dot16(w, s)
            o = _dot16(qc * egc, s) + _dot16(attn, v_new)
            if lower:
                acc_ref[rows, :] = o
            else:
                acc_ref[rows, :] += o
            return s * jnp.exp(g_last) + _dot16_tn(kc * jnp.exp(g_last - gc), v_new)

        s0 = s0_ref[0, d, 0] if latent else jnp.zeros((DN_DK, DN_DV), F32)
        s_fin = lax.fori_loop(0, n, chunk, s0)
        if not latent:
            sn_ref[0, d, 0] = s_fin

    o = acc_ref[...]
    o = o * lax.rsqrt(jnp.mean(o * o, axis=-1, keepdims=True) + EPS) * nw_ref[...]
    z = z_ref[...]
    o_ref[...] = (o * (z * jax.nn.sigmoid(z))).astype(BF16)


def _deltanet(proj, ab, row0, nseq, L, latent, dn_conv, gp, dn_norm, s0=None):
    rb = row0 // L
    cb = C_DQKV // DN_DK
    col = lambda k: pl.BlockSpec((L, DN_DK), lambda b, h: (rb + b, cb + k * DN_HEADS + h))
    cw = lambda k: pl.BlockSpec((3, DN_DK), lambda b, h: (0, k * DN_HEADS + h))
    in_specs = [col(0), col(1), col(2),
                pl.BlockSpec((L, DN_DV), lambda b, h: (rb + b, C_DZ // DN_DV + h)),
                pl.BlockSpec((L, LANES), lambda b, h: (rb + b, 0)),
                cw(0), cw(1), cw(2),
                pl.BlockSpec((2, 2, DN_HEADS, 1, LANES), lambda b, h: (0, 0, 0, 0, 0)),
                pl.BlockSpec((1, DN_DV), lambda b, h: (0, 0))]
    args = [proj, proj, proj, proj, ab, dn_conv, dn_conv, dn_conv, gp, dn_norm.reshape(1, DN_DV)]
    out_specs = [pl.BlockSpec((L, DN_DV), lambda b, h: (b, h))]
    out_shape = [jax.ShapeDtypeStruct((nseq * L, DN_W), BF16)]
    st_spec = pl.BlockSpec((1, 2, 1, DN_DK, DN_DV), lambda b, h: (b, 0, h, 0, 0))
    if latent:
        in_specs.append(st_spec)
        args.append(s0)
    else:
        out_specs.append(st_spec)
        out_shape.append(jax.ShapeDtypeStruct((nseq, 2, DN_HEADS, DN_DK, DN_DV), F32))
    return pl.pallas_call(
        functools.partial(_dn_kernel, L=L, latent=latent),
        grid=(nseq, DN_HEADS),
        in_specs=in_specs, out_specs=out_specs, out_shape=out_shape,
        scratch_shapes=[pltpu.VMEM((L, DN_DV), F32)] * 4,
        compiler_params=_cparams(("arbitrary", "arbitrary")),
        name=f"deltanet_{L}",
    )(*args)


def _prep_weights(w_in, p_ret, p_hy, p_dn, w_o, w_up, ffn_conv, w_down):
    w_main = jnp.concatenate([w_in[:, :O_DA], w_in[:, O_MG:]], axis=1).astype(BF16)
    w_ab = jnp.pad(w_in[:, O_DA:O_MG], ((0, 0), (0, LANES - (O_MG - O_DA)))).astype(BF16)
    padc = FFN_PAD - FFN_DIM
    wu = jnp.concatenate([jnp.pad(w_up[:, :FFN_DIM], ((0, 0), (0, padc))),
                          jnp.pad(w_up[:, FFN_DIM:], ((0, 0), (0, padc)))], axis=1).astype(BF16)
    cw = jnp.concatenate([jnp.pad(ffn_conv[:, :FFN_DIM], ((0, 0), (0, padc))),
                          jnp.pad(ffn_conv[:, FFN_DIM:], ((0, 0), (0, padc)))], axis=1)
    wd = jnp.pad(w_down, ((0, padc), (0, 0))).astype(BF16)
    return (w_main, w_ab, p_ret.astype(BF16), p_hy.astype(BF16), p_dn.astype(BF16),
            w_o.astype(BF16), wu, cw, wd)


def kernel(x_prompt, x_sample, state_ret, state_dn, c, c_ctx, w_ada, b_ada, norm1, w_in, ret_decay, hy_short, hy_w1, hy_b1, hy_freq1, hy_w2, hy_b2, hy_freq2, hy_w3, hy_bias, dn_conv, dn_a_log, dn_dt_bias, dn_norm, p_ret, p_hy, p_dn, w_o, norm2, w_up, ffn_conv, w_down, norm_f):
    x = jnp.concatenate([x_prompt.reshape(MP, D_MODEL), x_sample.reshape(MS, D_MODEL)], axis=0)
    cond = jnp.concatenate([c_ctx[None], c, jnp.zeros((3, D_MODEL), F32)], axis=0)
    mod_all = _adaln(cond, w_ada, b_ada)
    new_ret, new_dn = [], []
    dft_p, dft_s = _dft_mats(SEQ), _dft_mats(DEC_SEQ)
    for l in range(DEPTH):
        (w_main, w_ab, pr, ph, pd, wo, wu, cw, wd) = _prep_weights(
            w_in[l], p_ret[l], p_hy[l], p_dn[l], w_o[l], w_up[l], ffn_conv[l], w_down[l])
        mod = mod_all[l].reshape(8, 1, N_MOD * D_MODEL)
        proj, ab = _proj_in(x, norm1[l].reshape(1, D_MODEL), mod, w_main, w_ab)

        dec = jnp.broadcast_to(ret_decay[l][:, :, None, None], (2, RET_HEADS, 1, LANES))
        rp, s_r = _retention(proj, 0, BATCH, SEQ, False, dec)
        (rs,) = _retention(proj, MP, DEC_BATCH, DEC_SEQ, True, dec, state_ret[:, l])
        filt = (hy_w1[l], hy_b1[l], hy_freq1[l], hy_w2[l], hy_b2[l], hy_freq2[l], hy_w3[l])
        hcp, hsp = _hy_filter(SEQ, HY_TC_P, dft_p[0], *filt)
        hcs, hss = _hy_filter(DEC_SEQ, HY_TC_S, dft_s[0], *filt)
        hp = _hy_conv(proj, 0, BATCH, SEQ, HY_TC_P, hy_short[l], hy_bias[l], hcp, hsp, *dft_p)
        hs = _hy_conv(proj, MP, DEC_BATCH, DEC_SEQ, HY_TC_S, hy_short[l], hy_bias[l], hcs, hss, *dft_s)
        gp = _dn_gate_params(dn_a_log[l], dn_dt_bias[l])
        dp, s_d = _deltanet(proj, ab, 0, BATCH, SEQ, False, dn_conv[l], gp, dn_norm[l])
        (ds,) = _deltanet(proj, ab, MP, DEC_BATCH, DEC_SEQ, True, dn_conv[l], gp, dn_norm[l],
                          state_dn[:, l])
        new_ret.append(s_r)
        new_dn.append(s_d)
        cat = lambda a, b: jnp.concatenate([a, b], axis=0)
        mix = _merge(cat(rp, rs), cat(hp, hs), cat(dp, ds), pr, ph, pd, proj)
        x = _resid_matmul(mix, wo, x, mod, 2, "attn_out")
        act = _ffn_up(x, norm2[l].reshape(1, D_MODEL), mod, wu, cw)
        x = _resid_matmul(act, wd, x, mod, 5, "ffn_down")
    y = _final_norm(x, norm_f.reshape(1, D_MODEL))
    y_prompt = y[:MP].reshape(BATCH, SEQ, D_MODEL)
    y_sample = y[MP:].reshape(DEC_BATCH, DEC_SEQ, D_MODEL)
    return (y_prompt, y_sample, jnp.stack(new_ret, axis=1), jnp.stack(new_dn, axis=1))
```

```python
import functools
import math

import jax
import jax.numpy as jnp
import numpy as np
from jax import lax
from jax.experimental import pallas as pl
from jax.experimental.pallas import tpu as pltpu

F32 = jnp.float32
BF16 = jnp.bfloat16

D_MODEL = 2048
BATCH = 32
SEQ = 256
DEPTH = 2
DEC_BATCH = 4
DEC_SEQ = 1024
GRID_W = 64
EPS = 1e-6
ROPE_BASE = 10000.0
RET_HEADS = 8
RET_DK = 64
RET_DV = 128
RET_QK = RET_HEADS * RET_DK
RET_W = RET_HEADS * RET_DV
RET_CHUNK = 128
HY_W = 1024
HY_ORDER = 2
HY_EMB = 33
HY_HID = 64
HY_TARGET = 1e-2
HY_FAST = 0.3
HY_SLOW = 1.5
DN_HEADS = 8
DN_DK = 128
DN_DV = 128
DN_QK = DN_HEADS * DN_DK
DN_W = DN_HEADS * DN_DV
DN_CHUNK = 64
N_BRANCH = 3
FFN_DIM = 5504
N_MOD = 6

MP = BATCH * SEQ
MS = DEC_BATCH * DEC_SEQ
M_TOK = MP + MS

LANES = 128
TM = 1024
TN = 512
TN_WIDE = 1024
FFN_PAD = 5632
HY_TC_P = 1024
HY_TC_S = 256
VMEM_LIMIT = 56 * 1024 * 1024

C_RQ, C_RK, C_RV, C_RG = 0, 512, 1024, 2048
C_HY, C_DQKV, C_DZ, C_MG = 3072, 6144, 9216, 10240
N_MAIN = 16384
O_DA, O_MG = 10240, 10272


def _cparams(sem):
    return pltpu.CompilerParams(dimension_semantics=sem, vmem_limit_bytes=VMEM_LIMIT)


def _mod_row(i):
    return jnp.where(i < MP // TM, 0, 1 + (i - MP // TM) // (DEC_SEQ // TM))


def _mod_kernel(cond_ref, w_ref, b_ref, o_ref):
    c = cond_ref[...]
    s = (c * jax.nn.sigmoid(c)).astype(BF16)
    o_ref[0] = jnp.dot(s, w_ref[0].astype(BF16), preferred_element_type=F32) + b_ref[0]


def _adaln(cond, w_ada, b_ada):
    tn = 1024
    return pl.pallas_call(
        _mod_kernel,
        grid=(DEPTH, N_MOD * D_MODEL // tn),
        in_specs=[pl.BlockSpec((8, D_MODEL), lambda l, j: (0, 0)),
                  pl.BlockSpec((1, D_MODEL, tn), lambda l, j: (l, 0, j)),
                  pl.BlockSpec((1, 1, tn), lambda l, j: (l, 0, j))],
        out_specs=pl.BlockSpec((1, 8, tn), lambda l, j: (l, 0, j)),
        out_shape=jax.ShapeDtypeStruct((DEPTH, 8, N_MOD * D_MODEL), F32),
        compiler_params=_cparams(("arbitrary", "arbitrary")),
        name="adaln_mod",
    )(cond, w_ada, b_ada.reshape(DEPTH, 1, N_MOD * D_MODEL))


def _modulated_norm(x, nw, sc, sh):
    var = jnp.mean(x * x, axis=-1, keepdims=True)
    return (x * lax.rsqrt(var + EPS) * nw) * (1.0 + sc) + sh


def _proj_in_kernel(x_ref, nw_ref, sh_ref, sc_ref, w_ref, wab_ref, o_ref, ab_ref, h_ref):
    @pl.when(pl.program_id(1) == 0)
    def _():
        h = _modulated_norm(x_ref[...], nw_ref[...], sc_ref[0], sh_ref[0]).astype(BF16)
        h_ref[...] = h
        ab_ref[...] = jnp.dot(h, wab_ref[...], preferred_element_type=F32)

    o_ref[...] = jnp.dot(h_ref[...], w_ref[...], preferred_element_type=F32)


def _proj_in(x, nw, mod, w_main, w_ab):
    return pl.pallas_call(
        _proj_in_kernel,
        grid=(M_TOK // TM, N_MAIN // TN_WIDE),
        in_specs=[pl.BlockSpec((TM, D_MODEL), lambda i, j: (i, 0)),
                  pl.BlockSpec((1, D_MODEL), lambda i, j: (0, 0)),
                  pl.BlockSpec((1, 1, D_MODEL), lambda i, j: (_mod_row(i), 0, 0)),
                  pl.BlockSpec((1, 1, D_MODEL), lambda i, j: (_mod_row(i), 0, 1)),
                  pl.BlockSpec((D_MODEL, TN_WIDE), lambda i, j: (0, j)),
                  pl.BlockSpec((D_MODEL, LANES), lambda i, j: (0, 0))],
        out_specs=[pl.BlockSpec((TM, TN_WIDE), lambda i, j: (i, j)),
                   pl.BlockSpec((TM, LANES), lambda i, j: (i, 0))],
        out_shape=[jax.ShapeDtypeStruct((M_TOK, N_MAIN), F32),
                   jax.ShapeDtypeStruct((M_TOK, LANES), F32)],
        scratch_shapes=[pltpu.VMEM((TM, D_MODEL), BF16)],
        compiler_params=_cparams(("arbitrary", "arbitrary")),
        name="proj_in",
    )(x, nw, mod, mod, w_main, w_ab)


def _merge_kernel(r_ref, h_ref, d_ref, pr_ref, ph_ref, pd_ref, gr_ref, gh_ref, gd_ref, o_ref):
    mix = jax.nn.sigmoid(gr_ref[...]) * jnp.dot(r_ref[...], pr_ref[...], preferred_element_type=F32)
    mix += jax.nn.sigmoid(gh_ref[...]) * jnp.dot(h_ref[...], ph_ref[...], preferred_element_type=F32)
    mix += jax.nn.sigmoid(gd_ref[...]) * jnp.dot(d_ref[...], pd_ref[...], preferred_element_type=F32)
    o_ref[...] = mix.astype(BF16)


def _merge(ret_out, hy_out, dn_out, p_ret, p_hy, p_dn, proj):
    g0 = C_MG // TN
    gstep = D_MODEL // TN
    bspec = pl.BlockSpec((TM, RET_W), lambda i, j: (i, 0))
    pspec = pl.BlockSpec((RET_W, TN), lambda i, j: (0, j))
    return pl.pallas_call(
        _merge_kernel,
        grid=(M_TOK // TM, D_MODEL // TN),
        in_specs=[bspec, bspec, bspec, pspec, pspec, pspec,
                  pl.BlockSpec((TM, TN), lambda i, j: (i, g0 + j)),
                  pl.BlockSpec((TM, TN), lambda i, j: (i, g0 + gstep + j)),
                  pl.BlockSpec((TM, TN), lambda i, j: (i, g0 + 2 * gstep + j))],
        out_specs=pl.BlockSpec((TM, TN), lambda i, j: (i, j)),
        out_shape=jax.ShapeDtypeStruct((M_TOK, D_MODEL), BF16),
        compiler_params=_cparams(("arbitrary", "arbitrary")),
        name="branch_merge",
    )(ret_out, hy_out, dn_out, p_ret, p_hy, p_dn, proj, proj, proj)


def _resid_kernel(a_ref, w_ref, x_ref, g_ref, o_ref):
    o_ref[...] = x_ref[...] + g_ref[0] * jnp.dot(a_ref[...], w_ref[...], preferred_element_type=F32)


def _resid_matmul(a, w, x, mod, gate_idx, tn, name):
    k = a.shape[1]
    gb = gate_idx * (D_MODEL // tn)
    return pl.pallas_call(
        _resid_kernel,
        grid=(M_TOK // TM, D_MODEL // tn),
        in_specs=[pl.BlockSpec((TM, k), lambda i, j: (i, 0)),
                  pl.BlockSpec((k, tn), lambda i, j: (0, j)),
                  pl.BlockSpec((TM, tn), lambda i, j: (i, j)),
                  pl.BlockSpec((1, 1, tn), lambda i, j: (_mod_row(i), 0, gb + j))],
        out_specs=pl.BlockSpec((TM, tn), lambda i, j: (i, j)),
        out_shape=jax.ShapeDtypeStruct((M_TOK, D_MODEL), F32),
        compiler_params=_cparams(("arbitrary", "arbitrary")),
        name=name,
    )(a, w, x, mod)


PAD_ROWS = 8


def _ffn_up_kernel(x_ref, nw_ref, sh_ref, sc_ref, wa_ref, wb_ref, ca_ref, cb_ref, o_ref,
                   h_ref, ua_ref, ub_ref):
    i = pl.program_id(0)

    @pl.when(pl.program_id(1) == 0)
    def _():
        h_ref[...] = _modulated_norm(x_ref[...], nw_ref[...], sc_ref[0], sh_ref[0]).astype(BF16)
        zeros = jnp.zeros((PAD_ROWS, TN), F32)
        for ref in (ua_ref, ub_ref):
            ref[pl.ds(0, PAD_ROWS), :] = zeros
            ref[pl.ds(PAD_ROWS + TM, PAD_ROWS), :] = zeros

    h = h_ref[...]
    ua_ref[pl.ds(PAD_ROWS, TM), :] = jnp.dot(h, wa_ref[...], preferred_element_type=F32)
    ub_ref[pl.ds(PAD_ROWS, TM), :] = jnp.dot(h, wb_ref[...], preferred_element_type=F32)

    seq = jnp.where(i < MP // TM, SEQ, DEC_SEQ)
    pos = lax.broadcasted_iota(jnp.int32, (TM, 1), 0) & (seq - 1)
    has_prev = pos != 0
    has_next = pos != seq - 1

    def conv(u_ref, c_ref):
        prev = jnp.where(has_prev, u_ref[pl.ds(PAD_ROWS - 1, TM), :], 0.0)
        nxt = jnp.where(has_next, u_ref[pl.ds(PAD_ROWS + 1, TM), :], 0.0)
        mid = u_ref[pl.ds(PAD_ROWS, TM), :]
        return prev * c_ref[0:1, :] + mid * c_ref[1:2, :] + nxt * c_ref[2:3, :]

    ga = conv(ua_ref, ca_ref)
    gb = conv(ub_ref, cb_ref)
    o_ref[...] = (ga * jax.nn.sigmoid(ga) * gb).astype(BF16)


def _ffn_up(x, nw, mod, w_up, conv_w):
    nj = FFN_PAD // TN
    return pl.pallas_call(
        _ffn_up_kernel,
        grid=(M_TOK // TM, nj),
        in_specs=[pl.BlockSpec((TM, D_MODEL), lambda i, j: (i, 0)),
                  pl.BlockSpec((1, D_MODEL), lambda i, j: (0, 0)),
                  pl.BlockSpec((1, 1, D_MODEL), lambda i, j: (_mod_row(i), 0, 3)),
                  pl.BlockSpec((1, 1, D_MODEL), lambda i, j: (_mod_row(i), 0, 4)),
                  pl.BlockSpec((D_MODEL, TN), lambda i, j: (0, j)),
                  pl.BlockSpec((D_MODEL, TN), lambda i, j: (0, nj + j)),
                  pl.BlockSpec((3, TN), lambda i, j: (0, j)),
                  pl.BlockSpec((3, TN), lambda i, j: (0, nj + j))],
        out_specs=pl.BlockSpec((TM, TN), lambda i, j: (i, j)),
        out_shape=jax.ShapeDtypeStruct((M_TOK, FFN_PAD), BF16),
        scratch_shapes=[pltpu.VMEM((TM, D_MODEL), BF16),
                        pltpu.VMEM((TM + 2 * PAD_ROWS, TN), F32),
                        pltpu.VMEM((TM + 2 * PAD_ROWS, TN), F32)],
        compiler_params=_cparams(("arbitrary", "arbitrary")),
        name="ffn_up",
    )(x, nw, mod, mod, w_up, w_up, conv_w, conv_w)


def _final_norm_kernel(x_ref, w_ref, o_ref):
    x = x_ref[...]
    var = jnp.mean(x * x, axis=-1, keepdims=True)
    o_ref[...] = x * lax.rsqrt(var + EPS) * w_ref[...]


def _final_norm(x, w, row0, rows):
    rb = row0 // TM
    return pl.pallas_call(
        _final_norm_kernel,
        grid=(rows // TM,),
        in_specs=[pl.BlockSpec((TM, D_MODEL), lambda i: (rb + i, 0)),
                  pl.BlockSpec((1, D_MODEL), lambda i: (0, 0))],
        out_specs=pl.BlockSpec((TM, D_MODEL), lambda i: (i, 0)),
        out_shape=jax.ShapeDtypeStruct((rows, D_MODEL), F32),
        compiler_params=_cparams(("arbitrary",)),
        name=f"final_norm_{row0}",
    )(x, w)


def _dft_mats(L):
    n = 2 * L
    t = np.arange(L)
    ang = 2.0 * np.pi * ((t[:, None] * t[None, :]) % n) / n
    fc, fs = np.cos(ang), np.sin(ang)
    alt = (-1.0) ** t
    fs[0, :] = alt
    fwd = np.concatenate([fc, fs], axis=0)
    w = np.full((L, 1), 2.0)
    w[0] = 1.0
    gc = (fc * w / n).T
    gs = (fs * 2.0 / n).T
    gs[:, 0] = alt / n
    inv = np.concatenate([gc, gs], axis=1)
    return jnp.asarray(fwd, BF16), jnp.asarray(inv, BF16)


def _hy_feats(L):
    t = jnp.linspace(0.0, 1.0, L, dtype=F32)[:, None]
    bands = (HY_EMB - 1) // 2
    wpos = 2.0 * math.pi * jnp.arange(L, dtype=F32)[:, None] / L
    fr = jnp.linspace(1e-4, bands - 1, bands, dtype=F32)[None, :]
    feats = jnp.concatenate([t, jnp.cos(fr * wpos), -jnp.sin(fr * wpos)], axis=-1)
    return jnp.pad(feats, ((0, 0), (0, LANES - HY_EMB))), t


def _hy_filter_kernel(feats_ref, t_ref, w1_ref, b1_ref, f1_ref, w2_ref, b2_ref, f2_ref,
                      w3f_ref, w3b_ref, delta_ref, fwd_ref, hc_ref, hs_ref, *, L):
    hid = jnp.dot(feats_ref[...].astype(BF16), w1_ref[...].astype(BF16), preferred_element_type=F32)
    hid = jnp.sin(f1_ref[...] * (hid + b1_ref[...]))
    hid = jnp.dot(hid.astype(BF16), w2_ref[...].astype(BF16), preferred_element_type=F32)
    hid = jnp.sin(f2_ref[...] * (hid + b2_ref[...])).astype(BF16)
    win = jnp.exp(-t_ref[...] * delta_ref[...])

    def spectrum(w3_ref):
        h = jnp.dot(hid, w3_ref[...].astype(BF16), preferred_element_type=F32) * win
        h = h / (jnp.sum(jnp.abs(h), axis=0, keepdims=True) + EPS)
        return jnp.dot(fwd_ref[...], h.astype(BF16), preferred_element_type=F32)

    a = spectrum(w3f_ref)
    b = spectrum(w3b_ref)
    dc = lax.broadcasted_iota(jnp.int32, (L, 1), 0) == 0
    hc_ref[0] = a[:L] + b[:L]
    hs_ref[0] = a[L:] + jnp.where(dc, b[L:], -b[L:])


def _hy_filter(L, tc, fwd, w1, b1, f1, w2, b2, f2, w3):
    feats, t = _hy_feats(L)
    deltas = jnp.abs(jnp.linspace(math.log(HY_TARGET) / HY_FAST, math.log(HY_TARGET) / HY_SLOW,
                                  HY_W, dtype=F32))[None, :]
    w1p = jnp.pad(w1, ((0, LANES - HY_EMB), (0, 0)))
    nj = HY_W // tc
    full = lambda shape: pl.BlockSpec(shape, lambda o, j: (0,) * len(shape))
    out = jax.ShapeDtypeStruct((HY_ORDER, L, HY_W), F32)
    return pl.pallas_call(
        functools.partial(_hy_filter_kernel, L=L),
        grid=(HY_ORDER, nj),
        in_specs=[full((L, LANES)), full((L, 1)), full((LANES, HY_HID)), full((1, HY_HID)),
                  full((1, HY_HID)), full((HY_HID, HY_HID)), full((1, HY_HID)), full((1, HY_HID)),
                  pl.BlockSpec((HY_HID, tc), lambda o, j: (0, o * nj + j)),
                  pl.BlockSpec((HY_HID, tc), lambda o, j: (0, (HY_ORDER + o) * nj + j)),
                  pl.BlockSpec((1, tc), lambda o, j: (0, j)),
                  full((2 * L, L))],
        out_specs=[pl.BlockSpec((1, L, tc), lambda o, j: (o, 0, j)),
                   pl.BlockSpec((1, L, tc), lambda o, j: (o, 0, j))],
        out_shape=[out, out],
        compiler_params=_cparams(("arbitrary", "arbitrary")),
        name=f"hyena_filter_{L}",
    )(feats, t, w1p, b1.reshape(1, HY_HID), f1.reshape(1, HY_HID), w2, b2.reshape(1, HY_HID),
      f2.reshape(1, HY_HID), w3, w3, deltas, fwd)


def _hy_conv_kernel(v_ref, x1_ref, x2_ref, sv_ref, s1_ref, s2_ref, bias_ref, hc_ref, hs_ref,
                    fwd_ref, inv_ref, o_ref, *, L):
    row = lax.broadcasted_iota(jnp.int32, (L, 1), 0)
    first = row == 0
    last = row == L - 1

    def short_conv(u_ref, s_ref):
        u = u_ref[...]
        prev = jnp.where(first, 0.0, pltpu.roll(u, 1, 0))
        nxt = jnp.where(last, 0.0, pltpu.roll(u, L - 1, 0))
        return prev * s_ref[0:1, :] + u * s_ref[1:2, :] + nxt * s_ref[2:3, :]

    def long_conv(z, order):
        spec = jnp.dot(fwd_ref[...], z.astype(BF16), preferred_element_type=F32)
        zc, zs = spec[:L], spec[L:]
        hc, hs = hc_ref[order], hs_ref[order]
        yc = zc * hc - jnp.where(first, 0.0, zs * hs)
        ys = jnp.where(first, zs * hs, zc * hs + zs * hc)
        y = jnp.dot(inv_ref[...], jnp.concatenate([yc, ys], axis=0).astype(BF16),
                    preferred_element_type=F32)
        return y + z * bias_ref[order:order + 1, :]

    z = short_conv(x1_ref, s1_ref) * long_conv(short_conv(v_ref, sv_ref), 0)
    o_ref[...] = (short_conv(x2_ref, s2_ref) * long_conv(z, 1)).astype(BF16)


def _hy_conv(proj, row0, nseq, L, tc, hy_short, hy_bias, hc, hs, fwd, inv):
    nj = HY_W // tc
    rb = row0 // L
    cb = C_HY // tc
    col = lambda k: pl.BlockSpec((L, tc), lambda b, j: (rb + b, cb + k * nj + j))
    sw = lambda k: pl.BlockSpec((3, tc), lambda b, j: (0, k * nj + j))
    spec3 = pl.BlockSpec((HY_ORDER, L, tc), lambda b, j: (0, 0, j))
    return pl.pallas_call(
        functools.partial(_hy_conv_kernel, L=L),
        grid=(nseq, nj),
        in_specs=[col(0), col(1), col(2), sw(0), sw(1), sw(2),
                  pl.BlockSpec((HY_ORDER, tc), lambda b, j: (0, j)), spec3, spec3,
                  pl.BlockSpec((2 * L, L), lambda b, j: (0, 0)),
                  pl.BlockSpec((L, 2 * L), lambda b, j: (0, 0))],
        out_specs=pl.BlockSpec((L, tc), lambda b, j: (b, j)),
        out_shape=jax.ShapeDtypeStruct((nseq * L, HY_W), BF16),
        compiler_params=_cparams(("arbitrary", "arbitrary")),
        name=f"hyena_conv_{L}",
    )(proj, proj, proj, hy_short, hy_short, hy_short, hy_bias, hc, hs, fwd, inv)


def _rope_tables(L):
    n_rows = L // GRID_W
    pos_r = jnp.repeat(jnp.arange(n_rows, dtype=F32), GRID_W)
    pos_c = jnp.tile(jnp.arange(GRID_W, dtype=F32), n_rows)
    nf = RET_DK // 4
    inv = ROPE_BASE ** (-jnp.arange(nf, dtype=F32) / nf)
    ar = pos_r[:, None] * inv[None, :]
    ac = pos_c[:, None] * inv[None, :]
    cos = jnp.concatenate([jnp.cos(ar), jnp.cos(ar), jnp.cos(ac), jnp.cos(ac)], axis=-1)
    sin = jnp.concatenate([-jnp.sin(ar), jnp.sin(ar), -jnp.sin(ac), jnp.sin(ac)], axis=-1)
    return jnp.tile(cos, (1, 2)), jnp.tile(sin, (1, 2))


def _ret_kernel(*refs, L, latent):
    if latent:
        (q_ref, k_ref, v_ref, g_ref, dec_ref, cos_ref, sin_ref, s0_ref, o_ref,
         acc_ref, qs_ref, ks_ref) = refs
    else:
        q_ref, k_ref, v_ref, g_ref, dec_ref, o_ref, sn_ref, acc_ref, qs_ref, ks_ref = refs
    C = RET_CHUNK
    n = L // C
    lane = lax.broadcasted_iota(jnp.int32, (1, 2 * RET_DK), 1)
    q = q_ref[...]
    k = k_ref[...] * (RET_DK ** -0.5)
    if latent:
        nf = RET_DK // 4
        lo = (lane % (2 * nf)) < nf

        def rope(x):
            swapped = jnp.where(lo, pltpu.roll(x, 2 * RET_DK - nf, 1), pltpu.roll(x, nf, 1))
            return x * cos_ref[...] + swapped * sin_ref[...]

        q, k = rope(q), rope(k)
    qs_ref[...] = q
    ks_ref[...] = k
    ri = lax.broadcasted_iota(jnp.int32, (C, C), 0)
    ci = lax.broadcasted_iota(jnp.int32, (C, C), 1)
    rel = (ri - ci).astype(F32)
    rowf = ri.astype(F32)

    chains = [(h, d) for h in range(2) for d in range(2)]
    heads, vcols, dmats, q_decs, k_decs, c_decs, states = [], [], [], [], [], [], []
    for h, d in chains:
        dec = dec_ref[d, 2 * pl.program_id(1) + h]
        lg = jnp.minimum(dec, 0.0) - jnp.log1p(jnp.exp(-jnp.abs(dec)))
        if d == 0:
            dmats.append(jnp.where(rel >= 0, jnp.exp(rel * lg), 0.0))
            q_decs.append(jnp.exp((rowf + 1.0) * lg))
            k_decs.append(jnp.exp((C - 1.0 - rowf) * lg))
        else:
            dmats.append(jnp.where(rel <= 0, jnp.exp(-rel * lg), 0.0))
            q_decs.append(jnp.exp((C - rowf) * lg))
            k_decs.append(jnp.exp(rowf * lg))
        c_decs.append(jnp.exp(C * lg))
        heads.append((lane // RET_DK) == h)
        vcols.append(slice(h * RET_DV, (h + 1) * RET_DV))
        if latent:
            s_init = s0_ref[0, d, h]
            zero = jnp.zeros_like(s_init)
            states.append(jnp.concatenate([s_init, zero] if h == 0 else [zero, s_init], axis=0))
        else:
            states.append(jnp.zeros((2 * RET_DK, RET_DV), F32))

    nt = lambda a, b: lax.dot_general(a, b, (((1,), (1,)), ((), ())), preferred_element_type=F32)
    tn = lambda a, b: lax.dot_general(a, b, (((0,), (0,)), ((), ())), preferred_element_type=F32)
    nn = lambda a, b: jnp.dot(a, b, preferred_element_type=F32)
    for i in range(n):
        rows = [pl.ds((i if d == 0 else n - 1 - i) * C, C) for _, d in chains]
        qc = [jnp.where(hd, qs_ref[r, :], 0.0) for hd, r in zip(heads, rows)]
        kc = [ks_ref[r, :] for r in rows]
        vc = [v_ref[r, vc_].astype(BF16) for r, vc_ in zip(rows, vcols)]
        sc = [nt(q.astype(BF16), k.astype(BF16)) * dm for q, k, dm in zip(qc, kc, dmats)]
        qs = [nn((q * qd).astype(BF16), s.astype(BF16)) for q, qd, s in zip(qc, q_decs, states)]
        kv = [tn((k * kd).astype(BF16), v) for k, kd, v in zip(kc, k_decs, vc)]
        o = [nn(s_.astype(BF16), v) + q_ for s_, v, q_ in zip(sc, vc, qs)]
        for (h, d), r, vc_, o_ in zip(chains, rows, vcols, o):
            acc_ref[d, r, vc_] = o_
        states = [s * cd + k_ for s, cd, k_ in zip(states, c_decs, kv)]
    if not latent:
        for (h, d), s in zip(chains, states):
            sn_ref[0, d, h] = s[h * RET_DK:(h + 1) * RET_DK]

    for h in range(2):
        vcol = slice(h * RET_DV, (h + 1) * RET_DV)
        o = acc_ref[0, :, vcol] + acc_ref[1, :, vcol]
        o = o * lax.rsqrt(jnp.mean(o * o, axis=-1, keepdims=True) + EPS)
        g = g_ref[:, vcol]
        o_ref[:, vcol] = (o * (g * jax.nn.sigmoid(g))).astype(BF16)


def _retention(proj, row0, nseq, L, latent, dec, s0=None):
    rb = row0 // L
    npair = RET_HEADS // 2
    qk = lambda c0: pl.BlockSpec((L, 2 * RET_DK), lambda b, p: (rb + b, c0 // (2 * RET_DK) + p))
    vg = lambda c0: pl.BlockSpec((L, 2 * RET_DV), lambda b, p: (rb + b, c0 // (2 * RET_DV) + p))
    in_specs = [qk(C_RQ), qk(C_RK), vg(C_RV), vg(C_RG),
                pl.BlockSpec((2, RET_HEADS, 1, LANES), lambda b, p: (0, 0, 0, 0))]
    args = [proj, proj, proj, proj, dec]
    out_specs = [pl.BlockSpec((L, 2 * RET_DV), lambda b, p: (b, p))]
    out_shape = [jax.ShapeDtypeStruct((nseq * L, RET_W), BF16)]
    st_spec = pl.BlockSpec((1, 2, 2, RET_DK, RET_DV), lambda b, p: (b, 0, p, 0, 0))
    if latent:
        cos, sin = _rope_tables(L)
        tab = pl.BlockSpec((L, 2 * RET_DK), lambda b, p: (0, 0))
        in_specs += [tab, tab, st_spec]
        args += [cos, sin, s0]
    else:
        out_specs.append(st_spec)
        out_shape.append(jax.ShapeDtypeStruct((nseq, 2, RET_HEADS, RET_DK, RET_DV), F32))
    return pl.pallas_call(
        functools.partial(_ret_kernel, L=L, latent=latent),
        grid=(nseq, npair),
        in_specs=in_specs, out_specs=out_specs, out_shape=out_shape,
        scratch_shapes=[pltpu.VMEM((2, L, 2 * RET_DV), F32),
                        pltpu.VMEM((L, 2 * RET_DK), F32),
                        pltpu.VMEM((L, 2 * RET_DK), F32)],
        compiler_params=_cparams(("arbitrary", "arbitrary")),
        name=f"retention_{L}",
    )(*args)


DN_C = 128
DN_HP = 2
DN_GROUP = 2


def _dn_gate_params(a_log, dt_bias):
    gp = jnp.stack([a_log.reshape(-1), dt_bias.reshape(-1)], axis=0)
    return jnp.pad(gp, ((0, 0), (0, LANES - 2 * DN_HEADS)))


def _dot16(a, b):
    return jnp.dot(a.astype(BF16), b.astype(BF16), preferred_element_type=F32)


def _dot16_nt(a, b):
    return lax.dot_general(a.astype(BF16), b.astype(BF16), (((1,), (1,)), ((), ())),
                           preferred_element_type=F32)


def _dot16_tn(a, b):
    return lax.dot_general(a.astype(BF16), b.astype(BF16), (((0,), (0,)), ((), ())),
                           preferred_element_type=F32)


def _split3(x):
    p1 = x.astype(BF16)
    r = x - p1.astype(F32)
    p2 = r.astype(BF16)
    return p1, p2, (r - p2.astype(F32)).astype(BF16)


def _dot3(parts, rhs16):
    return sum(jnp.dot(p, rhs16, preferred_element_type=F32) for p in parts)


def _dot3_rhs(lhs16, parts):
    return sum(jnp.dot(lhs16, p, preferred_element_type=F32) for p in parts)


def _block_rows(c, size):
    if isinstance(c, int):
        return pl.ds(c * size, size)
    return pl.ds(pl.multiple_of(c * size, size), size)


def _dn_kernel(*refs, L, latent):
    (q_ref, k_ref, v_ref, z_ref, ab_ref, cq_ref, ck_ref, cv_ref, gp_ref, nw_ref) = refs[:10]
    if latent:
        s0_ref, o_ref = refs[10:12]
    else:
        o_ref, sn_ref = refs[10:12]
    (qs_ref, ks_ref, vs_ref, beta_ref, g_ref, u_ref, acc_ref,
     w_ref, at_ref, qd_ref, kdt_ref, egl_ref) = refs[12:]
    C = DN_C
    n = L // C
    HP = DN_HP
    row = lax.broadcasted_iota(jnp.int32, (L, 1), 0)
    first = row == 0
    last = row == L - 1

    def conv_silu(u_ref, c_ref, cols):
        u = u_ref[:, cols]
        prev = jnp.where(first, 0.0, pltpu.roll(u, 1, 0))
        nxt = jnp.where(last, 0.0, pltpu.roll(u, L - 1, 0))
        y = prev * c_ref[0:1, cols] + u * c_ref[1:2, cols] + nxt * c_ref[2:3, cols]
        return y * jax.nn.sigmoid(y)

    def l2n(x):
        return x * lax.rsqrt(jnp.sum(x * x, axis=-1, keepdims=True) + EPS)

    for hh in range(HP):
        cols = slice(hh * DN_DK, (hh + 1) * DN_DK)
        qs_ref[hh] = l2n(conv_silu(q_ref, cq_ref, cols)) * (DN_DK ** -0.5)
        ks_ref[hh] = l2n(conv_silu(k_ref, ck_ref, cols))
        vs_ref[hh] = conv_silu(v_ref, cv_ref, cols)

    ri = lax.broadcasted_iota(jnp.int32, (C, C), 0)
    ci = lax.broadcasted_iota(jnp.int32, (C, C), 1)
    rx = ri ^ ci
    eye = (ri == ci).astype(F32)

    ab = ab_ref[...]
    lane = lax.broadcasted_iota(jnp.int32, (1, LANES), 1)
    x = ab + gp_ref[1:2, :]
    g_all = -jnp.exp(gp_ref[0:1, :]) * (jnp.maximum(x, 0.0) + jnp.log1p(jnp.exp(-jnp.abs(x))))
    gates3 = _split3(jnp.where(lane < 2 * DN_HEADS, g_all, jax.nn.sigmoid(ab)))
    for hh in range(HP):
        head = HP * pl.program_id(1) + hh
        for d in range(2):
            sel_g = (ri == d * DN_HEADS + head).astype(BF16)
            sel_b = (ri == (2 + d) * DN_HEADS + head).astype(BF16)
            g_ref[2 * hh + d] = _dot3(gates3, sel_g)
            beta_ref[2 * hh + d] = _dot3(gates3[:2], sel_b)

    incl = [ri >= ci, ri <= ci]
    strict = [ri > ci, ri < ci]
    half_bit = [ri, ci]

    def quadrant(d, b):
        return (rx >= b // 2) & (rx < b) & ((half_bit[d] & (b // 2)) != 0)

    def prepare(insts):
        each = lambda fn, *cols: [fn(*a) for a in zip(*cols)]
        hs = [hh for hh, _, _ in insts]
        ds = [d for _, _, d in insts]
        hds = [2 * hh + d for hh, _, d in insts]
        rows = [_block_rows(c, C) for _, c, _ in insts]
        kc = each(lambda hh, r: ks_ref[hh, r, :], hs, rows)
        qc = each(lambda hh, r: qs_ref[hh, r, :], hs, rows)
        beta = each(lambda hd, r: beta_ref[hd, r, :], hds, rows)
        gc = each(lambda d, hd, r: _dot3_rhs(incl[d].astype(BF16), _split3(g_ref[hd, r, :])),
                  ds, hds, rows)
        decay = each(lambda d, g: jnp.where(incl[d], jnp.exp(jnp.where(incl[d], g - g.T, 0.0)), 0.0),
                     ds, gc)
        kb = each(lambda k, b_: k * b_, kc, beta)
        kk = each(_dot16_nt, kb, kc)
        a_mat = each(lambda d, x, dc: jnp.where(strict[d], x * dc, 0.0), ds, kk, decay)
        inv = each(lambda d, a: eye - jnp.where(quadrant(d, 2), a, 0.0), ds, a_mat)
        b = 4
        while b <= C:
            off = each(lambda d, a: jnp.where(quadrant(d, b), a, 0.0), ds, a_mat)
            t = each(_dot16, inv, off)
            t = each(_dot16, t, inv)
            inv = each(lambda x, y: x - y, inv, t)
            b *= 2
        egc = [jnp.exp(g) for g in gc]
        rhs = each(lambda hh, r, b_, k, e: jnp.concatenate([vs_ref[hh, r, :] * b_, k * e], axis=1),
                   hs, rows, beta, kb, egc)
        sol = each(_dot16, inv, rhs)
        qk = each(_dot16_nt, qc, kc)
        for (_, c, d), hd, r, s, a, dc, g, e, q, k in zip(insts, hds, rows, sol, qk, decay, gc, egc,
                                                         qc, kc):
            g_last = g[C - 1:C, :] if d == 0 else g[0:1, :]
            u_ref[hd, r, :] = s[:, :DN_DV]
            w_ref[hd, r, :] = s[:, DN_DV:].astype(BF16)
            at_ref[hd, r, :] = (a * dc).astype(BF16)
            qd_ref[hd, r, :] = (q * e).astype(BF16)
            kdt_ref[hd, r, :] = (k * jnp.exp(g_last - g)).T.astype(BF16)
            egl_ref[hd, _block_rows(c, 8), :] = jnp.broadcast_to(jnp.exp(g_last), (8, LANES))

    group = min(n, DN_GROUP)

    def prepare_group(j, carry):
        prepare([(hh, group * j + t, d) for hh in range(HP) for t in range(group) for d in range(2)])
        return carry

    if n == group:
        prepare_group(0, 0)
    else:
        lax.fori_loop(0, n // group, prepare_group, 0)

    chains = [(hh, d) for hh in range(HP) for d in range(2)]

    def advance(i, states):
        cs = [i if d == 0 else n - 1 - i for _, d in chains]
        rows = [_block_rows(c, C) for c in cs]
        hds = [2 * hh + d for hh, d in chains]
        dot = lambda a, b: jnp.dot(a, b, preferred_element_type=F32)
        s16 = [s.astype(BF16) for s in states]
        ws = [dot(w_ref[hd, r, :], s) for hd, r, s in zip(hds, rows, s16)]
        qs = [dot(qd_ref[hd, r, :], s) for hd, r, s in zip(hds, rows, s16)]
        v16 = [(u_ref[hd, r, :] - w).astype(BF16) for hd, r, w in zip(hds, rows, ws)]
        av = [dot(at_ref[hd, r, :], v) for hd, r, v in zip(hds, rows, v16)]
        kv = [dot(kdt_ref[hd, r, :], v) for hd, r, v in zip(hds, rows, v16)]
        new_states = []
        for hd, r, c, s, q, a, k in zip(hds, rows, cs, states, qs, av, kv):
            acc_ref[hd, r, :] = q + a
            new_states.append(s * egl_ref[hd, _block_rows(c, 8), :][0:1, :] + k)
        return tuple(new_states)

    if latent:
        init = tuple(s0_ref[0, d, hh] for hh, d in chains)
    else:
        init = (jnp.zeros((DN_DK, DN_DV), F32),) * len(chains)
    if n == 2:
        fin = advance(1, advance(0, init))
    else:
        fin = lax.fori_loop(0, n, advance, init)
    if not latent:
        for (hh, d), s in zip(chains, fin):
            sn_ref[0, d, hh] = s

    for hh in range(HP):
        cols = slice(hh * DN_DV, (hh + 1) * DN_DV)
        o = acc_ref[2 * hh] + acc_ref[2 * hh + 1]
        o = o * lax.rsqrt(jnp.mean(o * o, axis=-1, keepdims=True) + EPS) * nw_ref[...]
        z = z_ref[:, cols]
        o_ref[:, cols] = (o * (z * jax.nn.sigmoid(z))).astype(BF16)


def _deltanet(proj, ab, row0, nseq, L, latent, dn_conv, gp, dn_norm, s0=None):
    rb = row0 // L
    wide = DN_HP * DN_DK
    cb = C_DQKV // wide
    ng = DN_HEADS // DN_HP
    col = lambda k: pl.BlockSpec((L, wide), lambda b, h: (rb + b, cb + k * ng + h))
    cw = lambda k: pl.BlockSpec((3, wide), lambda b, h: (0, k * ng + h))
    in_specs = [col(0), col(1), col(2),
                pl.BlockSpec((L, wide), lambda b, h: (rb + b, C_DZ // wide + h)),
                pl.BlockSpec((L, LANES), lambda b, h: (rb + b, 0)),
                cw(0), cw(1), cw(2),
                pl.BlockSpec((2, LANES), lambda b, h: (0, 0)),
                pl.BlockSpec((1, DN_DV), lambda b, h: (0, 0))]
    args = [proj, proj, proj, proj, ab, dn_conv, dn_conv, dn_conv, gp, dn_norm.reshape(1, DN_DV)]
    out_specs = [pl.BlockSpec((L, wide), lambda b, h: (b, h))]
    out_shape = [jax.ShapeDtypeStruct((nseq * L, DN_W), BF16)]
    st_spec = pl.BlockSpec((1, 2, DN_HP, DN_DK, DN_DV), lambda b, h: (b, 0, h, 0, 0))
    if latent:
        in_specs.append(st_spec)
        args.append(s0)
    else:
        out_specs.append(st_spec)
        out_shape.append(jax.ShapeDtypeStruct((nseq, 2, DN_HEADS, DN_DK, DN_DV), F32))
    nslot = 2 * DN_HP
    return pl.pallas_call(
        functools.partial(_dn_kernel, L=L, latent=latent),
        grid=(nseq, ng),
        in_specs=in_specs, out_specs=out_specs, out_shape=out_shape,
        scratch_shapes=([pltpu.VMEM((DN_HP, L, DN_DK), F32)] * 3
                        + [pltpu.VMEM((nslot, L, DN_DV), F32)] * 4
                        + [pltpu.VMEM((nslot, L, DN_DV), BF16)] * 4
                        + [pltpu.VMEM((nslot, 8 * (L // DN_C), LANES), F32)]),
        compiler_params=_cparams(("arbitrary", "arbitrary")),
        name=f"deltanet_{L}",
    )(*args)


def _prep_weights(w_in, p_ret, p_hy, p_dn, w_o, w_up, ffn_conv, w_down):
    w_main = jnp.concatenate([w_in[:, :O_DA], w_in[:, O_MG:]], axis=1).astype(BF16)
    w_ab = jnp.pad(w_in[:, O_DA:O_MG], ((0, 0), (0, LANES - (O_MG - O_DA)))).astype(BF16)
    padc = FFN_PAD - FFN_DIM
    wu = jnp.concatenate([jnp.pad(w_up[:, :FFN_DIM], ((0, 0), (0, padc))),
                          jnp.pad(w_up[:, FFN_DIM:], ((0, 0), (0, padc)))], axis=1).astype(BF16)
    cw = jnp.concatenate([jnp.pad(ffn_conv[:, :FFN_DIM], ((0, 0), (0, padc))),
                          jnp.pad(ffn_conv[:, FFN_DIM:], ((0, 0), (0, padc)))], axis=1)
    wd = jnp.pad(w_down, ((0, padc), (0, 0))).astype(BF16)
    return (w_main, w_ab, p_ret.astype(BF16), p_hy.astype(BF16), p_dn.astype(BF16),
            w_o.astype(BF16), wu, cw, wd)


def kernel(x_prompt, x_sample, state_ret, state_dn, c, c_ctx, w_ada, b_ada, norm1, w_in, ret_decay, hy_short, hy_w1, hy_b1, hy_freq1, hy_w2, hy_b2, hy_freq2, hy_w3, hy_bias, dn_conv, dn_a_log, dn_dt_bias, dn_norm, p_ret, p_hy, p_dn, w_o, norm2, w_up, ffn_conv, w_down, norm_f):
    x = jnp.concatenate([x_prompt.reshape(MP, D_MODEL), x_sample.reshape(MS, D_MODEL)], axis=0)
    cond = jnp.concatenate([c_ctx[None], c, jnp.zeros((3, D_MODEL), F32)], axis=0)
    mod_all = _adaln(cond, w_ada, b_ada)
    new_ret, new_dn = [], []
    dft_p, dft_s = _dft_mats(SEQ), _dft_mats(DEC_SEQ)
    for l in range(DEPTH):
        (w_main, w_ab, pr, ph, pd, wo, wu, cw, wd) = _prep_weights(
            w_in[l], p_ret[l], p_hy[l], p_dn[l], w_o[l], w_up[l], ffn_conv[l], w_down[l])
        mod = mod_all[l].reshape(8, 1, N_MOD * D_MODEL)
        proj, ab = _proj_in(x, norm1[l].reshape(1, D_MODEL), mod, w_main, w_ab)

        dec = jnp.broadcast_to(ret_decay[l][:, :, None, None], (2, RET_HEADS, 1, LANES))
        rp, s_r = _retention(proj, 0, BATCH, SEQ, False, dec)
        (rs,) = _retention(proj, MP, DEC_BATCH, DEC_SEQ, True, dec, state_ret[:, l])
        filt = (hy_w1[l], hy_b1[l], hy_freq1[l], hy_w2[l], hy_b2[l], hy_freq2[l], hy_w3[l])
        hcp, hsp = _hy_filter(SEQ, HY_TC_P, dft_p[0], *filt)
        hcs, hss = _hy_filter(DEC_SEQ, HY_TC_S, dft_s[0], *filt)
        hp = _hy_conv(proj, 0, BATCH, SEQ, HY_TC_P, hy_short[l], hy_bias[l], hcp, hsp, *dft_p)
        hs = _hy_conv(proj, MP, DEC_BATCH, DEC_SEQ, HY_TC_S, hy_short[l], hy_bias[l], hcs, hss, *dft_s)
        gp = _dn_gate_params(dn_a_log[l], dn_dt_bias[l])
        dp, s_d = _deltanet(proj, ab, 0, BATCH, SEQ, False, dn_conv[l], gp, dn_norm[l])
        (ds,) = _deltanet(proj, ab, MP, DEC_BATCH, DEC_SEQ, True, dn_conv[l], gp, dn_norm[l],
                          state_dn[:, l])
        new_ret.append(s_r)
        new_dn.append(s_d)
        cat = lambda a, b: jnp.concatenate([a, b], axis=0)
        mix = _merge(cat(rp, rs), cat(hp, hs), cat(dp, ds), pr, ph, pd, proj)
        x = _resid_matmul(mix, wo, x, mod, 2, TN_WIDE, "attn_out")
        act = _ffn_up(x, norm2[l].reshape(1, D_MODEL), mod, wu, cw)
        x = _resid_matmul(act, wd, x, mod, 5, TN, "ffn_down")
    nf = norm_f.reshape(1, D_MODEL)
    y_prompt = _final_norm(x, nf, 0, MP).reshape(BATCH, SEQ, D_MODEL)
    y_sample = _final_norm(x, nf, MP, MS).reshape(DEC_BATCH, DEC_SEQ, D_MODEL)
    return (y_prompt, y_sample, jnp.stack(new_ret, axis=1), jnp.stack(new_dn, axis=1))
```

```python
import functools
import math

import jax
import jax.numpy as jnp
import numpy as np
from jax import lax
from jax.experimental import pallas as pl
from jax.experimental.pallas import tpu as pltpu

F32 = jnp.float32
BF16 = jnp.bfloat16

D_MODEL = 2048
BATCH = 32
SEQ = 256
DEPTH = 2
DEC_BATCH = 4
DEC_SEQ = 1024
GRID_W = 64
EPS = 1e-6
ROPE_BASE = 10000.0
RET_HEADS = 8
RET_DK = 64
RET_DV = 128
RET_QK = RET_HEADS * RET_DK
RET_W = RET_HEADS * RET_DV
RET_CHUNK = 128
HY_W = 1024
HY_ORDER = 2
HY_EMB = 33
HY_HID = 64
HY_TARGET = 1e-2
HY_FAST = 0.3
HY_SLOW = 1.5
DN_HEADS = 8
DN_DK = 128
DN_DV = 128
DN_QK = DN_HEADS * DN_DK
DN_W = DN_HEADS * DN_DV
DN_CHUNK = 64
N_BRANCH = 3
FFN_DIM = 5504
N_MOD = 6

MP = BATCH * SEQ
MS = DEC_BATCH * DEC_SEQ
M_TOK = MP + MS

LANES = 128
TM = 1024
TN = 512
TN_WIDE = 1024
FFN_PAD = 5632
HY_TC_P = 1024
HY_TC_S = 256
VMEM_LIMIT = 56 * 1024 * 1024

C_RQ, C_RK, C_RV, C_RG = 0, 512, 1024, 2048
C_HY, C_DQKV, C_DZ, C_MG = 3072, 6144, 9216, 10240
N_MAIN = 16384
O_DA, O_MG = 10240, 10272


def _cparams(sem):
    return pltpu.CompilerParams(dimension_semantics=sem, vmem_limit_bytes=VMEM_LIMIT)


def _mod_row(i):
    return jnp.where(i < MP // TM, 0, 1 + (i - MP // TM) // (DEC_SEQ // TM))


def _mod_kernel(cond_ref, w_ref, b_ref, o_ref):
    c = cond_ref[...]
    s = (c * jax.nn.sigmoid(c)).astype(BF16)
    o_ref[0] = jnp.dot(s, w_ref[0].astype(BF16), preferred_element_type=F32) + b_ref[0]


def _adaln(cond, w_ada, b_ada):
    tn = 1024
    return pl.pallas_call(
        _mod_kernel,
        grid=(DEPTH, N_MOD * D_MODEL // tn),
        in_specs=[pl.BlockSpec((8, D_MODEL), lambda l, j: (0, 0)),
                  pl.BlockSpec((1, D_MODEL, tn), lambda l, j: (l, 0, j)),
                  pl.BlockSpec((1, 1, tn), lambda l, j: (l, 0, j))],
        out_specs=pl.BlockSpec((1, 8, tn), lambda l, j: (l, 0, j)),
        out_shape=jax.ShapeDtypeStruct((DEPTH, 8, N_MOD * D_MODEL), F32),
        compiler_params=_cparams(("arbitrary", "arbitrary")),
        name="adaln_mod",
    )(cond, w_ada, b_ada.reshape(DEPTH, 1, N_MOD * D_MODEL))


def _modulated_norm(x, nw, sc, sh):
    var = jnp.mean(x * x, axis=-1, keepdims=True)
    return (x * lax.rsqrt(var + EPS) * nw) * (1.0 + sc) + sh


def _proj_in_kernel(x_ref, nw_ref, sh_ref, sc_ref, w_ref, wab_ref, o_ref, ab_ref, h_ref):
    @pl.when(pl.program_id(1) == 0)
    def _():
        h = _modulated_norm(x_ref[...], nw_ref[...], sc_ref[0], sh_ref[0]).astype(BF16)
        h_ref[...] = h
        ab_ref[...] = jnp.dot(h, wab_ref[...], preferred_element_type=F32)

    o_ref[...] = jnp.dot(h_ref[...], w_ref[...], preferred_element_type=F32)


def _proj_in(x, nw, mod, l, w_main, w_ab):
    return pl.pallas_call(
        _proj_in_kernel,
        grid=(M_TOK // TM, N_MAIN // TN_WIDE),
        in_specs=[pl.BlockSpec((TM, D_MODEL), lambda i, j: (i, 0)),
                  pl.BlockSpec((1, D_MODEL), lambda i, j: (0, 0)),
                  pl.BlockSpec((1, 1, D_MODEL), lambda i, j: (_mod_row(i), 0, 0)),
                  pl.BlockSpec((1, 1, D_MODEL), lambda i, j: (_mod_row(i), 0, 1)),
                  pl.BlockSpec((None, D_MODEL, TN_WIDE), lambda i, j: (l, 0, j)),
                  pl.BlockSpec((None, D_MODEL, LANES), lambda i, j: (l, 0, 0))],
        out_specs=[pl.BlockSpec((TM, TN_WIDE), lambda i, j: (i, j)),
                   pl.BlockSpec((TM, LANES), lambda i, j: (i, 0))],
        out_shape=[jax.ShapeDtypeStruct((M_TOK, N_MAIN), F32),
                   jax.ShapeDtypeStruct((M_TOK, LANES), F32)],
        scratch_shapes=[pltpu.VMEM((TM, D_MODEL), BF16)],
        compiler_params=_cparams(("arbitrary", "arbitrary")),
        name="proj_in",
    )(x, nw, mod, mod, w_main, w_ab)


def _merge_kernel(rp_ref, hp_ref, dp_ref, rs_ref, hs_ref, ds_ref, pr_ref, ph_ref, pd_ref,
                  gr_ref, gh_ref, gd_ref, o_ref):
    def mix(r_ref, h_ref, d_ref):
        m = jax.nn.sigmoid(gr_ref[...]) * jnp.dot(r_ref[...], pr_ref[...], preferred_element_type=F32)
        m += jax.nn.sigmoid(gh_ref[...]) * jnp.dot(h_ref[...], ph_ref[...], preferred_element_type=F32)
        m += jax.nn.sigmoid(gd_ref[...]) * jnp.dot(d_ref[...], pd_ref[...], preferred_element_type=F32)
        o_ref[...] = m.astype(BF16)

    is_context = pl.program_id(0) < MP // TM

    @pl.when(is_context)
    def _():
        mix(rp_ref, hp_ref, dp_ref)

    @pl.when(jnp.logical_not(is_context))
    def _():
        mix(rs_ref, hs_ref, ds_ref)


def _merge(ctx, lat, l, p_ret, p_hy, p_dn, proj):
    g0 = C_MG // TN
    gstep = D_MODEL // TN
    npt = MP // TM
    cspec = pl.BlockSpec((TM, RET_W), lambda i, j: (jnp.minimum(i, npt - 1), 0))
    lspec = pl.BlockSpec((TM, RET_W), lambda i, j: (jnp.maximum(i - npt, 0), 0))
    pspec = pl.BlockSpec((None, RET_W, TN), lambda i, j: (l, 0, j))
    return pl.pallas_call(
        _merge_kernel,
        grid=(M_TOK // TM, D_MODEL // TN),
        in_specs=[cspec, cspec, cspec, lspec, lspec, lspec, pspec, pspec, pspec,
                  pl.BlockSpec((TM, TN), lambda i, j: (i, g0 + j)),
                  pl.BlockSpec((TM, TN), lambda i, j: (i, g0 + gstep + j)),
                  pl.BlockSpec((TM, TN), lambda i, j: (i, g0 + 2 * gstep + j))],
        out_specs=pl.BlockSpec((TM, TN), lambda i, j: (i, j)),
        out_shape=jax.ShapeDtypeStruct((M_TOK, D_MODEL), BF16),
        compiler_params=_cparams(("arbitrary", "arbitrary")),
        name="branch_merge",
    )(*ctx, *lat, p_ret, p_hy, p_dn, proj, proj, proj)


def _resid_kernel(a_ref, w_ref, x_ref, g_ref, o_ref):
    o_ref[...] = x_ref[...] + g_ref[0] * jnp.dot(a_ref[...], w_ref[...], preferred_element_type=F32)


def _resid_matmul(a, l, w, x, mod, gate_idx, tn, name):
    k = a.shape[1]
    gb = gate_idx * (D_MODEL // tn)
    return pl.pallas_call(
        _resid_kernel,
        grid=(M_TOK // TM, D_MODEL // tn),
        in_specs=[pl.BlockSpec((TM, k), lambda i, j: (i, 0)),
                  pl.BlockSpec((None, k, tn), lambda i, j: (l, 0, j)),
                  pl.BlockSpec((TM, tn), lambda i, j: (i, j)),
                  pl.BlockSpec((1, 1, tn), lambda i, j: (_mod_row(i), 0, gb + j))],
        out_specs=pl.BlockSpec((TM, tn), lambda i, j: (i, j)),
        out_shape=jax.ShapeDtypeStruct((M_TOK, D_MODEL), F32),
        compiler_params=_cparams(("arbitrary", "arbitrary")),
        name=name,
    )(a, w, x, mod)


PAD_ROWS = 8


FFN_SUB = 256


def _ffn_up_kernel(x_ref, nw_ref, sh_ref, sc_ref, wa_ref, wb_ref, ca_ref, cb_ref, o_ref,
                   h_ref, *u_refs):
    i = pl.program_id(0)
    nsub = TN // FFN_SUB
    subs = [(slice(s * FFN_SUB, (s + 1) * FFN_SUB), u_refs[s], u_refs[nsub + s]) for s in range(nsub)]

    @pl.when(pl.program_id(1) == 0)
    def _():
        h_ref[...] = _modulated_norm(x_ref[...], nw_ref[...], sc_ref[0], sh_ref[0]).astype(BF16)
        zeros = jnp.zeros((PAD_ROWS, FFN_SUB), F32)
        for ref in u_refs:
            ref[pl.ds(0, PAD_ROWS), :] = zeros
            ref[pl.ds(PAD_ROWS + TM, PAD_ROWS), :] = zeros

    h = h_ref[...]
    for cols, ua_ref, ub_ref in subs:
        ua_ref[pl.ds(PAD_ROWS, TM), :] = jnp.dot(h, wa_ref[:, cols], preferred_element_type=F32)
        ub_ref[pl.ds(PAD_ROWS, TM), :] = jnp.dot(h, wb_ref[:, cols], preferred_element_type=F32)

    seq = jnp.where(i < MP // TM, SEQ, DEC_SEQ)
    pos = lax.broadcasted_iota(jnp.int32, (TM, 1), 0) & (seq - 1)
    has_prev = pos != 0
    has_next = pos != seq - 1

    def conv(u_ref, c_ref, cols):
        prev = jnp.where(has_prev, u_ref[pl.ds(PAD_ROWS - 1, TM), :], 0.0)
        nxt = jnp.where(has_next, u_ref[pl.ds(PAD_ROWS + 1, TM), :], 0.0)
        mid = u_ref[pl.ds(PAD_ROWS, TM), :]
        return prev * c_ref[0:1, cols] + mid * c_ref[1:2, cols] + nxt * c_ref[2:3, cols]

    for cols, ua_ref, ub_ref in subs:
        ga = conv(ua_ref, ca_ref, cols)
        gb = conv(ub_ref, cb_ref, cols)
        o_ref[:, cols] = (ga * jax.nn.sigmoid(ga) * gb).astype(BF16)


def _ffn_up(x, nw, mod, l, w_up, conv_w):
    nj = FFN_PAD // TN
    return pl.pallas_call(
        _ffn_up_kernel,
        grid=(M_TOK // TM, nj),
        in_specs=[pl.BlockSpec((TM, D_MODEL), lambda i, j: (i, 0)),
                  pl.BlockSpec((1, D_MODEL), lambda i, j: (0, 0)),
                  pl.BlockSpec((1, 1, D_MODEL), lambda i, j: (_mod_row(i), 0, 3)),
                  pl.BlockSpec((1, 1, D_MODEL), lambda i, j: (_mod_row(i), 0, 4)),
                  pl.BlockSpec((None, D_MODEL, TN), lambda i, j: (l, 0, j)),
                  pl.BlockSpec((None, D_MODEL, TN), lambda i, j: (l, 0, nj + j)),
                  pl.BlockSpec((None, 3, TN), lambda i, j: (l, 0, j)),
                  pl.BlockSpec((None, 3, TN), lambda i, j: (l, 0, nj + j))],
        out_specs=pl.BlockSpec((TM, TN), lambda i, j: (i, j)),
        out_shape=jax.ShapeDtypeStruct((M_TOK, FFN_PAD), BF16),
        scratch_shapes=([pltpu.VMEM((TM, D_MODEL), BF16)]
                        + [pltpu.VMEM((TM + 2 * PAD_ROWS, FFN_SUB), F32)] * (2 * (TN // FFN_SUB))),
        compiler_params=_cparams(("arbitrary", "arbitrary")),
        name="ffn_up",
    )(x, nw, mod, mod, w_up, w_up, conv_w, conv_w)


def _final_norm_kernel(x_ref, w_ref, o_ref):
    x = x_ref[...]
    var = jnp.mean(x * x, axis=-1, keepdims=True)
    o_ref[...] = x * lax.rsqrt(var + EPS) * w_ref[...]


def _final_norm(x, w, row0, rows):
    rb = row0 // TM
    return pl.pallas_call(
        _final_norm_kernel,
        grid=(rows // TM,),
        in_specs=[pl.BlockSpec((TM, D_MODEL), lambda i: (rb + i, 0)),
                  pl.BlockSpec((1, D_MODEL), lambda i: (0, 0))],
        out_specs=pl.BlockSpec((TM, D_MODEL), lambda i: (i, 0)),
        out_shape=jax.ShapeDtypeStruct((rows, D_MODEL), F32),
        compiler_params=_cparams(("arbitrary",)),
        name=f"final_norm_{row0}",
    )(x, w)


def _dft_mats(L):
    n = 2 * L
    t = np.arange(L)
    ang = 2.0 * np.pi * ((t[:, None] * t[None, :]) % n) / n
    fc, fs = np.cos(ang), np.sin(ang)
    alt = (-1.0) ** t
    fs[0, :] = alt
    fwd = np.concatenate([fc, fs], axis=0)
    w = np.full((L, 1), 2.0)
    w[0] = 1.0
    gc = (fc * w / n).T
    gs = (fs * 2.0 / n).T
    gs[:, 0] = alt / n
    inv = np.concatenate([gc, gs], axis=1)
    return jnp.asarray(fwd, BF16), jnp.asarray(inv, BF16)


def _hy_feats(L):
    t = jnp.linspace(0.0, 1.0, L, dtype=F32)[:, None]
    bands = (HY_EMB - 1) // 2
    wpos = 2.0 * math.pi * jnp.arange(L, dtype=F32)[:, None] / L
    fr = jnp.linspace(1e-4, bands - 1, bands, dtype=F32)[None, :]
    feats = jnp.concatenate([t, jnp.cos(fr * wpos), -jnp.sin(fr * wpos)], axis=-1)
    return jnp.pad(feats, ((0, 0), (0, LANES - HY_EMB))), t


def _hy_filter_kernel(feats_ref, t_ref, w1_ref, b1_ref, f1_ref, w2_ref, b2_ref, f2_ref,
                      w3f_ref, w3b_ref, delta_ref, fwd_ref, hc_ref, hs_ref, hid_ref, *, L):
    @pl.when((pl.program_id(0) == 0) & (pl.program_id(1) == 0))
    def _():
        hid = jnp.dot(feats_ref[...].astype(BF16), w1_ref[...].astype(BF16), preferred_element_type=F32)
        hid = jnp.sin(f1_ref[...] * (hid + b1_ref[...]))
        hid = jnp.dot(hid.astype(BF16), w2_ref[...].astype(BF16), preferred_element_type=F32)
        hid_ref[...] = jnp.sin(f2_ref[...] * (hid + b2_ref[...])).astype(BF16)

    hid = hid_ref[...]
    win = jnp.exp(-t_ref[...] * delta_ref[...])

    def spectrum(w3_ref):
        h = jnp.dot(hid, w3_ref[...].astype(BF16), preferred_element_type=F32) * win
        h = h / (jnp.sum(jnp.abs(h), axis=0, keepdims=True) + EPS)
        return jnp.dot(fwd_ref[...], h.astype(BF16), preferred_element_type=F32)

    a = spectrum(w3f_ref)
    b = spectrum(w3b_ref)
    dc = lax.broadcasted_iota(jnp.int32, (L, 1), 0) == 0
    hc_ref[0] = a[:L] + b[:L]
    hs_ref[0] = a[L:] + jnp.where(dc, b[L:], -b[L:])


def _hy_filter(L, tc, fwd, w1, b1, f1, w2, b2, f2, w3):
    feats, t = _hy_feats(L)
    deltas = jnp.abs(jnp.linspace(math.log(HY_TARGET) / HY_FAST, math.log(HY_TARGET) / HY_SLOW,
                                  HY_W, dtype=F32))[None, :]
    w1p = jnp.pad(w1, ((0, LANES - HY_EMB), (0, 0)))
    nj = HY_W // tc
    full = lambda shape: pl.BlockSpec(shape, lambda o, j: (0,) * len(shape))
    out = jax.ShapeDtypeStruct((HY_ORDER, L, HY_W), F32)
    return pl.pallas_call(
        functools.partial(_hy_filter_kernel, L=L),
        grid=(HY_ORDER, nj),
        in_specs=[full((L, LANES)), full((L, 1)), full((LANES, HY_HID)), full((1, HY_HID)),
                  full((1, HY_HID)), full((HY_HID, HY_HID)), full((1, HY_HID)), full((1, HY_HID)),
                  pl.BlockSpec((HY_HID, tc), lambda o, j: (0, o * nj + j)),
                  pl.BlockSpec((HY_HID, tc), lambda o, j: (0, (HY_ORDER + o) * nj + j)),
                  pl.BlockSpec((1, tc), lambda o, j: (0, j)),
                  full((2 * L, L))],
        out_specs=[pl.BlockSpec((1, L, tc), lambda o, j: (o, 0, j)),
                   pl.BlockSpec((1, L, tc), lambda o, j: (o, 0, j))],
        out_shape=[out, out],
        scratch_shapes=[pltpu.VMEM((L, HY_HID), BF16)],
        compiler_params=_cparams(("arbitrary", "arbitrary")),
        name=f"hyena_filter_{L}",
    )(feats, t, w1p, b1.reshape(1, HY_HID), f1.reshape(1, HY_HID), w2, b2.reshape(1, HY_HID),
      f2.reshape(1, HY_HID), w3, w3, deltas, fwd)


def _hy_conv_kernel(v_ref, x1_ref, x2_ref, sv_ref, s1_ref, s2_ref, bias_ref, hc_ref, hs_ref,
                    fwd_ref, inv_ref, o_ref, *, L):
    row = lax.broadcasted_iota(jnp.int32, (L, 1), 0)
    first = row == 0
    last = row == L - 1

    def short_conv(u_ref, s_ref):
        u = u_ref[...]
        prev = jnp.where(first, 0.0, pltpu.roll(u, 1, 0))
        nxt = jnp.where(last, 0.0, pltpu.roll(u, L - 1, 0))
        return prev * s_ref[0:1, :] + u * s_ref[1:2, :] + nxt * s_ref[2:3, :]

    def long_conv(z, order):
        spec = jnp.dot(fwd_ref[...], z.astype(BF16), preferred_element_type=F32)
        zc, zs = spec[:L], spec[L:]
        hc, hs = hc_ref[order], hs_ref[order]
        yc = zc * hc - jnp.where(first, 0.0, zs * hs)
        ys = jnp.where(first, zs * hs, zc * hs + zs * hc)
        y = jnp.dot(inv_ref[...], jnp.concatenate([yc, ys], axis=0).astype(BF16),
                    preferred_element_type=F32)
        return y + z * bias_ref[order:order + 1, :]

    z = short_conv(x1_ref, s1_ref) * long_conv(short_conv(v_ref, sv_ref), 0)
    o_ref[...] = (short_conv(x2_ref, s2_ref) * long_conv(z, 1)).astype(BF16)


def _hy_conv(proj, row0, nseq, L, tc, hy_short, hy_bias, hc, hs, fwd, inv):
    nj = HY_W // tc
    rb = row0 // L
    cb = C_HY // tc
    col = lambda k: pl.BlockSpec((L, tc), lambda b, j: (rb + b, cb + k * nj + j))
    sw = lambda k: pl.BlockSpec((3, tc), lambda b, j: (0, k * nj + j))
    spec3 = pl.BlockSpec((HY_ORDER, L, tc), lambda b, j: (0, 0, j))
    return pl.pallas_call(
        functools.partial(_hy_conv_kernel, L=L),
        grid=(nseq, nj),
        in_specs=[col(0), col(1), col(2), sw(0), sw(1), sw(2),
                  pl.BlockSpec((HY_ORDER, tc), lambda b, j: (0, j)), spec3, spec3,
                  pl.BlockSpec((2 * L, L), lambda b, j: (0, 0)),
                  pl.BlockSpec((L, 2 * L), lambda b, j: (0, 0))],
        out_specs=pl.BlockSpec((L, tc), lambda b, j: (b, j)),
        out_shape=jax.ShapeDtypeStruct((nseq * L, HY_W), BF16),
        compiler_params=_cparams(("arbitrary", "arbitrary")),
        name=f"hyena_conv_{L}",
    )(proj, proj, proj, hy_short, hy_short, hy_short, hy_bias, hc, hs, fwd, inv)


def _rope_tables(L):
    n_rows = L // GRID_W
    pos_r = jnp.repeat(jnp.arange(n_rows, dtype=F32), GRID_W)
    pos_c = jnp.tile(jnp.arange(GRID_W, dtype=F32), n_rows)
    nf = RET_DK // 4
    inv = ROPE_BASE ** (-jnp.arange(nf, dtype=F32) / nf)
    ar = pos_r[:, None] * inv[None, :]
    ac = pos_c[:, None] * inv[None, :]
    cos = jnp.concatenate([jnp.cos(ar), jnp.cos(ar), jnp.cos(ac), jnp.cos(ac)], axis=-1)
    sin = jnp.concatenate([-jnp.sin(ar), jnp.sin(ar), -jnp.sin(ac), jnp.sin(ac)], axis=-1)
    return jnp.tile(cos, (1, 2)), jnp.tile(sin, (1, 2))


def _ret_kernel(*refs, L, latent):
    if latent:
        (q_ref, k_ref, v_ref, g_ref, dec_ref, cos_ref, sin_ref, s0_ref, o_ref,
         acc_ref, qs_ref, ks_ref) = refs
    else:
        q_ref, k_ref, v_ref, g_ref, dec_ref, o_ref, sn_ref, acc_ref, qs_ref, ks_ref = refs
    C = RET_CHUNK
    n = L // C
    lane = lax.broadcasted_iota(jnp.int32, (1, 2 * RET_DK), 1)
    q = q_ref[...]
    k = k_ref[...] * (RET_DK ** -0.5)
    if latent:
        nf = RET_DK // 4
        lo = (lane % (2 * nf)) < nf

        def rope(x):
            swapped = jnp.where(lo, pltpu.roll(x, 2 * RET_DK - nf, 1), pltpu.roll(x, nf, 1))
            return x * cos_ref[...] + swapped * sin_ref[...]

        q, k = rope(q), rope(k)
    qs_ref[...] = q
    ks_ref[...] = k
    ri = lax.broadcasted_iota(jnp.int32, (C, C), 0)
    ci = lax.broadcasted_iota(jnp.int32, (C, C), 1)
    rel = (ri - ci).astype(F32)
    rowf = ri.astype(F32)

    chains = [(h, d) for h in range(2) for d in range(2)]
    heads, vcols, dmats, q_decs, k_decs, c_decs, states = [], [], [], [], [], [], []
    for h, d in chains:
        dec = dec_ref[d, 2 * pl.program_id(1) + h]
        lg = jnp.minimum(dec, 0.0) - jnp.log1p(jnp.exp(-jnp.abs(dec)))
        if d == 0:
            dmats.append(jnp.where(rel >= 0, jnp.exp(rel * lg), 0.0))
            q_decs.append(jnp.exp((rowf + 1.0) * lg))
            k_decs.append(jnp.exp((C - 1.0 - rowf) * lg))
        else:
            dmats.append(jnp.where(rel <= 0, jnp.exp(-rel * lg), 0.0))
            q_decs.append(jnp.exp((C - rowf) * lg))
            k_decs.append(jnp.exp(rowf * lg))
        c_decs.append(jnp.exp(C * lg))
        heads.append((lane // RET_DK) == h)
        vcols.append(slice(h * RET_DV, (h + 1) * RET_DV))
        if latent:
            s_init = s0_ref[0, d, h]
            zero = jnp.zeros_like(s_init)
            states.append(jnp.concatenate([s_init, zero] if h == 0 else [zero, s_init], axis=0))
        else:
            states.append(jnp.zeros((2 * RET_DK, RET_DV), F32))

    nt = lambda a, b: lax.dot_general(a, b, (((1,), (1,)), ((), ())), preferred_element_type=F32)
    tn = lambda a, b: lax.dot_general(a, b, (((0,), (0,)), ((), ())), preferred_element_type=F32)
    nn = lambda a, b: jnp.dot(a, b, preferred_element_type=F32)
    for i in range(n):
        rows = [pl.ds((i if d == 0 else n - 1 - i) * C, C) for _, d in chains]
        qc = [jnp.where(hd, qs_ref[r, :], 0.0) for hd, r in zip(heads, rows)]
        kc = [ks_ref[r, :] for r in rows]
        vc = [v_ref[r, vc_].astype(BF16) for r, vc_ in zip(rows, vcols)]
        sc = [nt(q.astype(BF16), k.astype(BF16)) * dm for q, k, dm in zip(qc, kc, dmats)]
        qs = [nn((q * qd).astype(BF16), s.astype(BF16)) for q, qd, s in zip(qc, q_decs, states)]
        kv = [tn((k * kd).astype(BF16), v) for k, kd, v in zip(kc, k_decs, vc)]
        o = [nn(s_.astype(BF16), v) + q_ for s_, v, q_ in zip(sc, vc, qs)]
        for (h, d), r, vc_, o_ in zip(chains, rows, vcols, o):
            acc_ref[d, r, vc_] = o_
        states = [s * cd + k_ for s, cd, k_ in zip(states, c_decs, kv)]
    if not latent:
        for (h, d), s in zip(chains, states):
            sn_ref[0, d, h] = s[h * RET_DK:(h + 1) * RET_DK]

    for h in range(2):
        vcol = slice(h * RET_DV, (h + 1) * RET_DV)
        o = acc_ref[0, :, vcol] + acc_ref[1, :, vcol]
        o = o * lax.rsqrt(jnp.mean(o * o, axis=-1, keepdims=True) + EPS)
        g = g_ref[:, vcol]
        o_ref[:, vcol] = (o * (g * jax.nn.sigmoid(g))).astype(BF16)


def _retention(proj, row0, nseq, L, latent, dec, s0=None):
    rb = row0 // L
    npair = RET_HEADS // 2
    qk = lambda c0: pl.BlockSpec((L, 2 * RET_DK), lambda b, p: (rb + b, c0 // (2 * RET_DK) + p))
    vg = lambda c0: pl.BlockSpec((L, 2 * RET_DV), lambda b, p: (rb + b, c0 // (2 * RET_DV) + p))
    in_specs = [qk(C_RQ), qk(C_RK), vg(C_RV), vg(C_RG),
                pl.BlockSpec((2, RET_HEADS, 1, LANES), lambda b, p: (0, 0, 0, 0))]
    args = [proj, proj, proj, proj, dec]
    out_specs = [pl.BlockSpec((L, 2 * RET_DV), lambda b, p: (b, p))]
    out_shape = [jax.ShapeDtypeStruct((nseq * L, RET_W), BF16)]
    st_spec = pl.BlockSpec((1, 2, 2, RET_DK, RET_DV), lambda b, p: (b, 0, p, 0, 0))
    if latent:
        cos, sin = _rope_tables(L)
        tab = pl.BlockSpec((L, 2 * RET_DK), lambda b, p: (0, 0))
        in_specs += [tab, tab, st_spec]
        args += [cos, sin, s0]
    else:
        out_specs.append(st_spec)
        out_shape.append(jax.ShapeDtypeStruct((nseq, 2, RET_HEADS, RET_DK, RET_DV), F32))
    return pl.pallas_call(
        functools.partial(_ret_kernel, L=L, latent=latent),
        grid=(nseq, npair),
        in_specs=in_specs, out_specs=out_specs, out_shape=out_shape,
        scratch_shapes=[pltpu.VMEM((2, L, 2 * RET_DV), F32),
                        pltpu.VMEM((L, 2 * RET_DK), F32),
                        pltpu.VMEM((L, 2 * RET_DK), F32)],
        compiler_params=_cparams(("arbitrary", "arbitrary")),
        name=f"retention_{L}",
    )(*args)


DN_C = 128
DN_HP_CTX = 4
DN_HP_LAT = 2
DN_GROUP = 2


def _dn_gate_params(a_log, dt_bias):
    gp = jnp.stack([a_log.reshape(-1), dt_bias.reshape(-1)], axis=0)
    return jnp.pad(gp, ((0, 0), (0, LANES - 2 * DN_HEADS)))


def _dot16(a, b):
    return jnp.dot(a.astype(BF16), b.astype(BF16), preferred_element_type=F32)


def _dot16_nt(a, b):
    return lax.dot_general(a.astype(BF16), b.astype(BF16), (((1,), (1,)), ((), ())),
                           preferred_element_type=F32)


def _dot16_tn(a, b):
    return lax.dot_general(a.astype(BF16), b.astype(BF16), (((0,), (0,)), ((), ())),
                           preferred_element_type=F32)


def _split3(x):
    p1 = x.astype(BF16)
    r = x - p1.astype(F32)
    p2 = r.astype(BF16)
    return p1, p2, (r - p2.astype(F32)).astype(BF16)


def _dot3(parts, rhs16):
    return sum(jnp.dot(p, rhs16, preferred_element_type=F32) for p in parts)


def _dot3_rhs(lhs16, parts):
    return sum(jnp.dot(lhs16, p, preferred_element_type=F32) for p in parts)


def _block_rows(c, size):
    if isinstance(c, int):
        return pl.ds(c * size, size)
    return pl.ds(pl.multiple_of(c * size, size), size)


def _dn_kernel(*refs, L, latent, hp):
    (q_ref, k_ref, v_ref, z_ref, ab_ref, cq_ref, ck_ref, cv_ref, gp_ref, nw_ref) = refs[:10]
    if latent:
        s0_ref, o_ref = refs[10:12]
    else:
        o_ref, sn_ref = refs[10:12]
    (qs_ref, ks_ref, vs_ref, beta_ref, g_ref, u_ref, acc_ref,
     w_ref, at_ref, qd_ref, kdt_ref, egl_ref) = refs[12:]
    C = DN_C
    n = L // C
    HP = hp
    row = lax.broadcasted_iota(jnp.int32, (L, 1), 0)
    first = row == 0
    last = row == L - 1

    def conv_silu(u_ref, c_ref, cols):
        u = u_ref[:, cols]
        prev = jnp.where(first, 0.0, pltpu.roll(u, 1, 0))
        nxt = jnp.where(last, 0.0, pltpu.roll(u, L - 1, 0))
        y = prev * c_ref[0:1, cols] + u * c_ref[1:2, cols] + nxt * c_ref[2:3, cols]
        return y * jax.nn.sigmoid(y)

    def l2n(x):
        return x * lax.rsqrt(jnp.sum(x * x, axis=-1, keepdims=True) + EPS)

    for hh in range(HP):
        cols = slice(hh * DN_DK, (hh + 1) * DN_DK)
        qs_ref[hh] = l2n(conv_silu(q_ref, cq_ref, cols)) * (DN_DK ** -0.5)
        ks_ref[hh] = l2n(conv_silu(k_ref, ck_ref, cols))
        vs_ref[hh] = conv_silu(v_ref, cv_ref, cols)

    ri = lax.broadcasted_iota(jnp.int32, (C, C), 0)
    ci = lax.broadcasted_iota(jnp.int32, (C, C), 1)
    rx = ri ^ ci
    eye = (ri == ci).astype(F32)

    ab = ab_ref[...]
    lane = lax.broadcasted_iota(jnp.int32, (1, LANES), 1)
    x = ab + gp_ref[1:2, :]
    g_all = -jnp.exp(gp_ref[0:1, :]) * (jnp.maximum(x, 0.0) + jnp.log1p(jnp.exp(-jnp.abs(x))))
    gates3 = _split3(jnp.where(lane < 2 * DN_HEADS, g_all, jax.nn.sigmoid(ab)))
    for hh in range(HP):
        head = HP * pl.program_id(1) + hh
        for d in range(2):
            sel_g = (ri == d * DN_HEADS + head).astype(BF16)
            sel_b = (ri == (2 + d) * DN_HEADS + head).astype(BF16)
            g_ref[2 * hh + d] = _dot3(gates3, sel_g)
            beta_ref[2 * hh + d] = _dot3(gates3[:2], sel_b)

    incl = [ri >= ci, ri <= ci]
    strict = [ri > ci, ri < ci]
    half_bit = [ri, ci]

    def quadrant(d, b):
        return (rx >= b // 2) & (rx < b) & ((half_bit[d] & (b // 2)) != 0)

    def prepare(insts):
        each = lambda fn, *cols: [fn(*a) for a in zip(*cols)]
        hs = [hh for hh, _, _ in insts]
        ds = [d for _, _, d in insts]
        hds = [2 * hh + d for hh, _, d in insts]
        rows = [_block_rows(c, C) for _, c, _ in insts]
        kc = each(lambda hh, r: ks_ref[hh, r, :], hs, rows)
        qc = each(lambda hh, r: qs_ref[hh, r, :], hs, rows)
        beta = each(lambda hd, r: beta_ref[hd, r, :], hds, rows)
        gc = each(lambda d, hd, r: _dot3_rhs(incl[d].astype(BF16), _split3(g_ref[hd, r, :])),
                  ds, hds, rows)
        decay = each(lambda d, g: jnp.where(incl[d], jnp.exp(jnp.where(incl[d], g - g.T, 0.0)), 0.0),
                     ds, gc)
        kb = each(lambda k, b_: k * b_, kc, beta)
        kk = each(_dot16_nt, kb, kc)
        a_mat = each(lambda d, x, dc: jnp.where(strict[d], x * dc, 0.0), ds, kk, decay)
        inv = each(lambda d, a: eye - jnp.where(quadrant(d, 2), a, 0.0), ds, a_mat)
        b = 4
        while b <= C:
            off = each(lambda d, a: jnp.where(quadrant(d, b), a, 0.0), ds, a_mat)
            t = each(_dot16, inv, off)
            t = each(_dot16, t, inv)
            inv = each(lambda x, y: x - y, inv, t)
            b *= 2
        egc = [jnp.exp(g) for g in gc]
        rhs = each(lambda hh, r, b_, k, e: jnp.concatenate([vs_ref[hh, r, :] * b_, k * e], axis=1),
                   hs, rows, beta, kb, egc)
        sol = each(_dot16, inv, rhs)
        qk = each(_dot16_nt, qc, kc)
        for (_, c, d), hd, r, s, a, dc, g, e, q, k in zip(insts, hds, rows, sol, qk, decay, gc, egc,
                                                         qc, kc):
            g_last = g[C - 1:C, :] if d == 0 else g[0:1, :]
            u_ref[hd, r, :] = s[:, :DN_DV]
            w_ref[hd, r, :] = s[:, DN_DV:].astype(BF16)
            at_ref[hd, r, :] = (a * dc).astype(BF16)
            qd_ref[hd, r, :] = (q * e).astype(BF16)
            kdt_ref[hd, r, :] = (k * jnp.exp(g_last - g)).T.astype(BF16)
            egl_ref[hd, _block_rows(c, 8), :] = jnp.broadcast_to(jnp.exp(g_last), (8, LANES))

    group = min(n, DN_GROUP)

    def prepare_group(j, carry):
        for h0 in range(0, HP, 2):
            prepare([(hh, group * j + t, d) for hh in (h0, h0 + 1) for t in range(group)
                     for d in range(2)])
        return carry

    if n == group:
        prepare_group(0, 0)
    else:
        lax.fori_loop(0, n // group, prepare_group, 0)

    chains = [(hh, d) for hh in range(HP) for d in range(2)]

    def advance(i, states):
        cs = [i if d == 0 else n - 1 - i for _, d in chains]
        rows = [_block_rows(c, C) for c in cs]
        hds = [2 * hh + d for hh, d in chains]
        dot = lambda a, b: jnp.dot(a, b, preferred_element_type=F32)
        s16 = [s.astype(BF16) for s in states]
        ws = [dot(w_ref[hd, r, :], s) for hd, r, s in zip(hds, rows, s16)]
        qs = [dot(qd_ref[hd, r, :], s) for hd, r, s in zip(hds, rows, s16)]
        v16 = [(u_ref[hd, r, :] - w).astype(BF16) for hd, r, w in zip(hds, rows, ws)]
        av = [dot(at_ref[hd, r, :], v) for hd, r, v in zip(hds, rows, v16)]
        kv = [dot(kdt_ref[hd, r, :], v) for hd, r, v in zip(hds, rows, v16)]
        new_states = []
        for hd, r, c, s, q, a, k in zip(hds, rows, cs, states, qs, av, kv):
            acc_ref[hd, r, :] = q + a
            new_states.append(s * egl_ref[hd, _block_rows(c, 8), :][0:1, :] + k)
        return tuple(new_states)

    if latent:
        init = tuple(s0_ref[0, d, hh] for hh, d in chains)
    else:
        init = (jnp.zeros((DN_DK, DN_DV), F32),) * len(chains)
    if n == 2:
        fin = advance(1, advance(0, init))
    else:
        fin = lax.fori_loop(0, n, advance, init)
    if not latent:
        for (hh, d), s in zip(chains, fin):
            sn_ref[0, d, hh] = s

    for hh in range(HP):
        cols = slice(hh * DN_DV, (hh + 1) * DN_DV)
        o = acc_ref[2 * hh] + acc_ref[2 * hh + 1]
        o = o * lax.rsqrt(jnp.mean(o * o, axis=-1, keepdims=True) + EPS) * nw_ref[...]
        z = z_ref[:, cols]
        o_ref[:, cols] = (o * (z * jax.nn.sigmoid(z))).astype(BF16)


def _deltanet(proj, ab, row0, nseq, L, latent, hp, dn_conv, gp, dn_norm, s0=None):
    rb = row0 // L
    wide = hp * DN_DK
    cb = C_DQKV // wide
    ng = DN_HEADS // hp
    col = lambda k: pl.BlockSpec((L, wide), lambda b, h: (rb + b, cb + k * ng + h))
    cw = lambda k: pl.BlockSpec((3, wide), lambda b, h: (0, k * ng + h))
    in_specs = [col(0), col(1), col(2),
                pl.BlockSpec((L, wide), lambda b, h: (rb + b, C_DZ // wide + h)),
                pl.BlockSpec((L, LANES), lambda b, h: (rb + b, 0)),
                cw(0), cw(1), cw(2),
                pl.BlockSpec((2, LANES), lambda b, h: (0, 0)),
                pl.BlockSpec((1, DN_DV), lambda b, h: (0, 0))]
    args = [proj, proj, proj, proj, ab, dn_conv, dn_conv, dn_conv, gp, dn_norm.reshape(1, DN_DV)]
    out_specs = [pl.BlockSpec((L, wide), lambda b, h: (b, h))]
    out_shape = [jax.ShapeDtypeStruct((nseq * L, DN_W), BF16)]
    st_spec = pl.BlockSpec((1, 2, hp, DN_DK, DN_DV), lambda b, h: (b, 0, h, 0, 0))
    if latent:
        in_specs.append(st_spec)
        args.append(s0)
    else:
        out_specs.append(st_spec)
        out_shape.append(jax.ShapeDtypeStruct((nseq, 2, DN_HEADS, DN_DK, DN_DV), F32))
    nslot = 2 * hp
    return pl.pallas_call(
        functools.partial(_dn_kernel, L=L, latent=latent, hp=hp),
        grid=(nseq, ng),
        in_specs=in_specs, out_specs=out_specs, out_shape=out_shape,
        scratch_shapes=([pltpu.VMEM((hp, L, DN_DK), F32)] * 3
                        + [pltpu.VMEM((nslot, L, DN_DV), F32)] * 4
                        + [pltpu.VMEM((nslot, L, DN_DV), BF16)] * 4
                        + [pltpu.VMEM((nslot, 8 * (L // DN_C), LANES), F32)]),
        compiler_params=_cparams(("arbitrary", "arbitrary")),
        name=f"deltanet_{L}",
    )(*args)


def _prep_weights(w_in, p_ret, p_hy, p_dn, w_o, w_up, ffn_conv, w_down):
    w_main = jnp.concatenate([w_in[:, :, :O_DA].astype(BF16), w_in[:, :, O_MG:].astype(BF16)], axis=2)
    w_ab = jnp.pad(w_in[:, :, O_DA:O_MG].astype(BF16), ((0, 0), (0, 0), (0, LANES - (O_MG - O_DA))))
    padc = ((0, 0), (0, 0), (0, FFN_PAD - FFN_DIM))
    wu = jnp.concatenate([jnp.pad(w_up[:, :, :FFN_DIM].astype(BF16), padc),
                          jnp.pad(w_up[:, :, FFN_DIM:].astype(BF16), padc)], axis=2)
    cw = jnp.concatenate([jnp.pad(ffn_conv[:, :, :FFN_DIM], padc),
                          jnp.pad(ffn_conv[:, :, FFN_DIM:], padc)], axis=2)
    wd = jnp.pad(w_down.astype(BF16), ((0, 0), (0, FFN_PAD - FFN_DIM), (0, 0)))
    return (w_main, w_ab, p_ret.astype(BF16), p_hy.astype(BF16), p_dn.astype(BF16),
            w_o.astype(BF16), wu, cw, wd)


def kernel(x_prompt, x_sample, state_ret, state_dn, c, c_ctx, w_ada, b_ada, norm1, w_in, ret_decay, hy_short, hy_w1, hy_b1, hy_freq1, hy_w2, hy_b2, hy_freq2, hy_w3, hy_bias, dn_conv, dn_a_log, dn_dt_bias, dn_norm, p_ret, p_hy, p_dn, w_o, norm2, w_up, ffn_conv, w_down, norm_f):
    x = jnp.concatenate([x_prompt.reshape(MP, D_MODEL), x_sample.reshape(MS, D_MODEL)], axis=0)
    cond = jnp.concatenate([c_ctx[None], c, jnp.zeros((3, D_MODEL), F32)], axis=0)
    mod_all = _adaln(cond, w_ada, b_ada)
    new_ret, new_dn = [], []
    dft_p, dft_s = _dft_mats(SEQ), _dft_mats(DEC_SEQ)
    (w_main, w_ab, pr, ph, pd, wo, wu, cw, wd) = _prep_weights(
        w_in, p_ret, p_hy, p_dn, w_o, w_up, ffn_conv, w_down)
    for l in range(DEPTH):
        mod = mod_all[l].reshape(8, 1, N_MOD * D_MODEL)
        proj, ab = _proj_in(x, norm1[l].reshape(1, D_MODEL), mod, l, w_main, w_ab)

        dec = jnp.broadcast_to(ret_decay[l][:, :, None, None], (2, RET_HEADS, 1, LANES))
        rp, s_r = _retention(proj, 0, BATCH, SEQ, False, dec)
        (rs,) = _retention(proj, MP, DEC_BATCH, DEC_SEQ, True, dec, state_ret[:, l])
        filt = (hy_w1[l], hy_b1[l], hy_freq1[l], hy_w2[l], hy_b2[l], hy_freq2[l], hy_w3[l])
        hcp, hsp = _hy_filter(SEQ, HY_TC_P, dft_p[0], *filt)
        hcs, hss = _hy_filter(DEC_SEQ, HY_TC_S, dft_s[0], *filt)
        hp = _hy_conv(proj, 0, BATCH, SEQ, HY_TC_P, hy_short[l], hy_bias[l], hcp, hsp, *dft_p)
        hs = _hy_conv(proj, MP, DEC_BATCH, DEC_SEQ, HY_TC_S, hy_short[l], hy_bias[l], hcs, hss, *dft_s)
        gp = _dn_gate_params(dn_a_log[l], dn_dt_bias[l])
        dp, s_d = _deltanet(proj, ab, 0, BATCH, SEQ, False, DN_HP_CTX, dn_conv[l], gp, dn_norm[l])
        (ds,) = _deltanet(proj, ab, MP, DEC_BATCH, DEC_SEQ, True, DN_HP_LAT, dn_conv[l], gp, dn_norm[l],
                          state_dn[:, l])
        new_ret.append(s_r)
        new_dn.append(s_d)
        mix = _merge((rp, hp, dp), (rs, hs, ds), l, pr, ph, pd, proj)
        x = _resid_matmul(mix, l, wo, x, mod, 2, TN_WIDE, "attn_out")
        act = _ffn_up(x, norm2[l].reshape(1, D_MODEL), mod, l, wu, cw)
        x = _resid_matmul(act, l, wd, x, mod, 5, TN, "ffn_down")
    nf = norm_f.reshape(1, D_MODEL)
    y_prompt = _final_norm(x, nf, 0, MP).reshape(BATCH, SEQ, D_MODEL)
    y_sample = _final_norm(x, nf, MP, MS).reshape(DEC_BATCH, DEC_SEQ, D_MODEL)
    return (y_prompt, y_sample, jnp.stack(new_ret, axis=1), jnp.stack(new_dn, axis=1))
```

```python
import functools
import math

import jax
import jax.numpy as jnp
import numpy as np
from jax import lax
from jax.experimental import pallas as pl
from jax.experimental.pallas import tpu as pltpu

F32 = jnp.float32
BF16 = jnp.bfloat16

D_MODEL = 2048
BATCH = 32
SEQ = 256
DEPTH = 2
DEC_BATCH = 4
DEC_SEQ = 1024
GRID_W = 64
EPS = 1e-6
ROPE_BASE = 10000.0
RET_HEADS = 8
RET_DK = 64
RET_DV = 128
RET_QK = RET_HEADS * RET_DK
RET_W = RET_HEADS * RET_DV
RET_CHUNK = 128
HY_W = 1024
HY_ORDER = 2
HY_EMB = 33
HY_HID = 64
HY_TARGET = 1e-2
HY_FAST = 0.3
HY_SLOW = 1.5
DN_HEADS = 8
DN_DK = 128
DN_DV = 128
DN_QK = DN_HEADS * DN_DK
DN_W = DN_HEADS * DN_DV
DN_CHUNK = 64
N_BRANCH = 3
FFN_DIM = 5504
N_MOD = 6

MP = BATCH * SEQ
MS = DEC_BATCH * DEC_SEQ
M_TOK = MP + MS

LANES = 128
TM = 1024
TN = 512
TN_WIDE = 1024
FFN_PAD = 5632
HY_TC_P = 1024
HY_TC_S = 256
VMEM_LIMIT = 56 * 1024 * 1024

C_RQ, C_RK, C_RV, C_RG = 0, 512, 1024, 2048
C_HY, C_DQKV, C_DZ, C_MG = 3072, 6144, 9216, 10240
N_MAIN = 16384
O_DA, O_MG = 10240, 10272


def _cparams(sem):
    return pltpu.CompilerParams(dimension_semantics=sem, vmem_limit_bytes=VMEM_LIMIT)


def _mod_row(i):
    return jnp.where(i < MP // TM, 0, 1 + (i - MP // TM) // (DEC_SEQ // TM))


def _mod_kernel(cond_ref, w_ref, b_ref, o_ref):
    c = cond_ref[...]
    s = (c * jax.nn.sigmoid(c)).astype(BF16)
    o_ref[0] = jnp.dot(s, w_ref[0].astype(BF16), preferred_element_type=F32) + b_ref[0]


def _adaln(cond, w_ada, b_ada):
    tn = 1024
    return pl.pallas_call(
        _mod_kernel,
        grid=(DEPTH, N_MOD * D_MODEL // tn),
        in_specs=[pl.BlockSpec((8, D_MODEL), lambda l, j: (0, 0)),
                  pl.BlockSpec((1, D_MODEL, tn), lambda l, j: (l, 0, j)),
                  pl.BlockSpec((1, 1, tn), lambda l, j: (l, 0, j))],
        out_specs=pl.BlockSpec((1, 8, tn), lambda l, j: (l, 0, j)),
        out_shape=jax.ShapeDtypeStruct((DEPTH, 8, N_MOD * D_MODEL), F32),
        compiler_params=_cparams(("arbitrary", "arbitrary")),
        name="adaln_mod",
    )(cond, w_ada, b_ada.reshape(DEPTH, 1, N_MOD * D_MODEL))


def _modulated_norm(x, nw, sc, sh):
    var = jnp.mean(x * x, axis=-1, keepdims=True)
    return (x * lax.rsqrt(var + EPS) * nw) * (1.0 + sc) + sh


def _proj_in_kernel(x_ref, nw_ref, sh_ref, sc_ref, wa_ref, wg_ref, wab_ref, o_ref, ab_ref, h_ref):
    j = pl.program_id(1)

    @pl.when(j == 0)
    def _():
        h = _modulated_norm(x_ref[...], nw_ref[...], sc_ref[0], sh_ref[0]).astype(BF16)
        h_ref[...] = h
        ab_ref[...] = jnp.dot(h, wab_ref[...], preferred_element_type=F32)

    @pl.when(j < NT_A)
    def _():
        o_ref[...] = jnp.dot(h_ref[...], wa_ref[...], preferred_element_type=F32)

    @pl.when(j >= NT_A)
    def _():
        o_ref[...] = jnp.dot(h_ref[...], wg_ref[...], preferred_element_type=F32)


NT_A = C_MG // TN_WIDE


def _proj_in(x, nw, mod, l, w_a, w_g, w_ab):
    return pl.pallas_call(
        _proj_in_kernel,
        grid=(M_TOK // TM, N_MAIN // TN_WIDE),
        in_specs=[pl.BlockSpec((TM, D_MODEL), lambda i, j: (i, 0)),
                  pl.BlockSpec((1, D_MODEL), lambda i, j: (0, 0)),
                  pl.BlockSpec((1, 1, D_MODEL), lambda i, j: (_mod_row(i), 0, 0)),
                  pl.BlockSpec((1, 1, D_MODEL), lambda i, j: (_mod_row(i), 0, 1)),
                  pl.BlockSpec((None, D_MODEL, TN_WIDE), lambda i, j: (l, 0, jnp.minimum(j, NT_A - 1))),
                  pl.BlockSpec((None, D_MODEL, TN_WIDE), lambda i, j: (l, 0, jnp.maximum(j - NT_A, 0))),
                  pl.BlockSpec((None, D_MODEL, LANES), lambda i, j: (l, 0, 0))],
        out_specs=[pl.BlockSpec((TM, TN_WIDE), lambda i, j: (i, j)),
                   pl.BlockSpec((TM, LANES), lambda i, j: (i, 0))],
        out_shape=[jax.ShapeDtypeStruct((M_TOK, N_MAIN), F32),
                   jax.ShapeDtypeStruct((M_TOK, LANES), F32)],
        scratch_shapes=[pltpu.VMEM((TM, D_MODEL), BF16)],
        compiler_params=_cparams(("arbitrary", "arbitrary")),
        name="proj_in",
    )(x, nw, mod, mod, w_a, w_g, w_ab)


def _merge_kernel(rp_ref, hp_ref, dp_ref, rs_ref, hs_ref, ds_ref, pr_ref, ph_ref, pd_ref,
                  gr_ref, gh_ref, gd_ref, o_ref):
    def mix(r_ref, h_ref, d_ref):
        m = jax.nn.sigmoid(gr_ref[...]) * jnp.dot(r_ref[...], pr_ref[...], preferred_element_type=F32)
        m += jax.nn.sigmoid(gh_ref[...]) * jnp.dot(h_ref[...], ph_ref[...], preferred_element_type=F32)
        m += jax.nn.sigmoid(gd_ref[...]) * jnp.dot(d_ref[...], pd_ref[...], preferred_element_type=F32)
        o_ref[...] = m.astype(BF16)

    is_context = pl.program_id(0) < MP // TM

    @pl.when(is_context)
    def _():
        mix(rp_ref, hp_ref, dp_ref)

    @pl.when(jnp.logical_not(is_context))
    def _():
        mix(rs_ref, hs_ref, ds_ref)


def _merge(ctx, lat, l, p_ret, p_hy, p_dn, proj):
    g0 = C_MG // TN
    gstep = D_MODEL // TN
    npt = MP // TM
    cspec = pl.BlockSpec((TM, RET_W), lambda i, j: (jnp.minimum(i, npt - 1), 0))
    lspec = pl.BlockSpec((TM, RET_W), lambda i, j: (jnp.maximum(i - npt, 0), 0))
    pspec = pl.BlockSpec((None, RET_W, TN), lambda i, j: (l, 0, j))
    return pl.pallas_call(
        _merge_kernel,
        grid=(M_TOK // TM, D_MODEL // TN),
        in_specs=[cspec, cspec, cspec, lspec, lspec, lspec, pspec, pspec, pspec,
                  pl.BlockSpec((TM, TN), lambda i, j: (i, g0 + j)),
                  pl.BlockSpec((TM, TN), lambda i, j: (i, g0 + gstep + j)),
                  pl.BlockSpec((TM, TN), lambda i, j: (i, g0 + 2 * gstep + j))],
        out_specs=pl.BlockSpec((TM, TN), lambda i, j: (i, j)),
        out_shape=jax.ShapeDtypeStruct((M_TOK, D_MODEL), BF16),
        compiler_params=_cparams(("arbitrary", "arbitrary")),
        name="branch_merge",
    )(*ctx, *lat, p_ret, p_hy, p_dn, proj, proj, proj)


def _resid_kernel(a_ref, w_ref, x_ref, g_ref, o_ref):
    o_ref[...] = x_ref[...] + g_ref[0] * jnp.dot(a_ref[...], w_ref[...], preferred_element_type=F32)


def _resid_matmul(a, l, w, x, mod, gate_idx, tn, name):
    k = a.shape[1]
    gb = gate_idx * (D_MODEL // tn)
    return pl.pallas_call(
        _resid_kernel,
        grid=(M_TOK // TM, D_MODEL // tn),
        in_specs=[pl.BlockSpec((TM, k), lambda i, j: (i, 0)),
                  pl.BlockSpec((None, k, tn), lambda i, j: (l, 0, j)),
                  pl.BlockSpec((TM, tn), lambda i, j: (i, j)),
                  pl.BlockSpec((1, 1, tn), lambda i, j: (_mod_row(i), 0, gb + j))],
        out_specs=pl.BlockSpec((TM, tn), lambda i, j: (i, j)),
        out_shape=jax.ShapeDtypeStruct((M_TOK, D_MODEL), F32),
        compiler_params=_cparams(("arbitrary", "arbitrary")),
        name=name,
    )(a, w, x, mod)


PAD_ROWS = 8


def _ffn_up_kernel(x_ref, nw_ref, sh_ref, sc_ref, wa_ref, wb_ref, ca_ref, cb_ref, o_ref,
                   h_ref, ua_ref, ub_ref):
    i = pl.program_id(0)

    @pl.when(pl.program_id(1) == 0)
    def _():
        h_ref[...] = _modulated_norm(x_ref[...], nw_ref[...], sc_ref[0], sh_ref[0]).astype(BF16)
        zeros = jnp.zeros((PAD_ROWS, TN), F32)
        for ref in (ua_ref, ub_ref):
            ref[pl.ds(0, PAD_ROWS), :] = zeros
            ref[pl.ds(PAD_ROWS + TM, PAD_ROWS), :] = zeros

    h = h_ref[...]
    ua_ref[pl.ds(PAD_ROWS, TM), :] = jnp.dot(h, wa_ref[...], preferred_element_type=F32)
    ub_ref[pl.ds(PAD_ROWS, TM), :] = jnp.dot(h, wb_ref[...], preferred_element_type=F32)

    seq = jnp.where(i < MP // TM, SEQ, DEC_SEQ)
    pos = lax.broadcasted_iota(jnp.int32, (TM, 1), 0) & (seq - 1)
    has_prev = pos != 0
    has_next = pos != seq - 1

    def conv(u_ref, c_ref):
        prev = jnp.where(has_prev, u_ref[pl.ds(PAD_ROWS - 1, TM), :], 0.0)
        nxt = jnp.where(has_next, u_ref[pl.ds(PAD_ROWS + 1, TM), :], 0.0)
        mid = u_ref[pl.ds(PAD_ROWS, TM), :]
        return prev * c_ref[0:1, :] + mid * c_ref[1:2, :] + nxt * c_ref[2:3, :]

    ga = conv(ua_ref, ca_ref)
    gb = conv(ub_ref, cb_ref)
    o_ref[...] = (ga * jax.nn.sigmoid(ga) * gb).astype(BF16)


def _ffn_up(x, nw, mod, l, w_gate, w_val, conv_w):
    nj = FFN_PAD // TN
    return pl.pallas_call(
        _ffn_up_kernel,
        grid=(M_TOK // TM, nj),
        in_specs=[pl.BlockSpec((TM, D_MODEL), lambda i, j: (i, 0)),
                  pl.BlockSpec((1, D_MODEL), lambda i, j: (0, 0)),
                  pl.BlockSpec((1, 1, D_MODEL), lambda i, j: (_mod_row(i), 0, 3)),
                  pl.BlockSpec((1, 1, D_MODEL), lambda i, j: (_mod_row(i), 0, 4)),
                  pl.BlockSpec((None, D_MODEL, TN), lambda i, j: (l, 0, j)),
                  pl.BlockSpec((None, D_MODEL, TN), lambda i, j: (l, 0, j)),
                  pl.BlockSpec((None, 3, TN), lambda i, j: (l, 0, j)),
                  pl.BlockSpec((None, 3, TN), lambda i, j: (l, 0, nj + j))],
        out_specs=pl.BlockSpec((TM, TN), lambda i, j: (i, j)),
        out_shape=jax.ShapeDtypeStruct((M_TOK, FFN_PAD), BF16),
        scratch_shapes=[pltpu.VMEM((TM, D_MODEL), BF16),
                        pltpu.VMEM((TM + 2 * PAD_ROWS, TN), F32),
                        pltpu.VMEM((TM + 2 * PAD_ROWS, TN), F32)],
        compiler_params=_cparams(("arbitrary", "arbitrary")),
        name="ffn_up",
    )(x, nw, mod, mod, w_gate, w_val, conv_w, conv_w)


def _final_norm_kernel(x_ref, w_ref, o_ref):
    x = x_ref[...]
    var = jnp.mean(x * x, axis=-1, keepdims=True)
    o_ref[...] = x * lax.rsqrt(var + EPS) * w_ref[...]


def _final_norm(x, w, row0, rows):
    rb = row0 // TM
    return pl.pallas_call(
        _final_norm_kernel,
        grid=(rows // TM,),
        in_specs=[pl.BlockSpec((TM, D_MODEL), lambda i: (rb + i, 0)),
                  pl.BlockSpec((1, D_MODEL), lambda i: (0, 0))],
        out_specs=pl.BlockSpec((TM, D_MODEL), lambda i: (i, 0)),
        out_shape=jax.ShapeDtypeStruct((rows, D_MODEL), F32),
        compiler_params=_cparams(("arbitrary",)),
        name=f"final_norm_{row0}",
    )(x, w)


def _dft_mats(L):
    n = 2 * L
    t = np.arange(L)
    ang = 2.0 * np.pi * ((t[:, None] * t[None, :]) % n) / n
    fc, fs = np.cos(ang), np.sin(ang)
    alt = (-1.0) ** t
    fs[0, :] = alt
    fwd = np.concatenate([fc, fs], axis=0)
    w = np.full((L, 1), 2.0)
    w[0] = 1.0
    gc = (fc * w / n).T
    gs = (fs * 2.0 / n).T
    gs[:, 0] = alt / n
    inv = np.concatenate([gc, gs], axis=1)
    return jnp.asarray(fwd, BF16), jnp.asarray(inv, BF16)


def _hy_feats(L):
    t = jnp.linspace(0.0, 1.0, L, dtype=F32)[:, None]
    bands = (HY_EMB - 1) // 2
    wpos = 2.0 * math.pi * jnp.arange(L, dtype=F32)[:, None] / L
    fr = jnp.linspace(1e-4, bands - 1, bands, dtype=F32)[None, :]
    feats = jnp.concatenate([t, jnp.cos(fr * wpos), -jnp.sin(fr * wpos)], axis=-1)
    return jnp.pad(feats, ((0, 0), (0, LANES - HY_EMB))), t


def _hy_filter_kernel(feats_ref, t_ref, w1_ref, b1_ref, f1_ref, w2_ref, b2_ref, f2_ref,
                      w3f_ref, w3b_ref, delta_ref, fwd_ref, hc_ref, hs_ref, hid_ref, *, L):
    @pl.when((pl.program_id(0) == 0) & (pl.program_id(1) == 0))
    def _():
        hid = jnp.dot(feats_ref[...].astype(BF16), w1_ref[...].astype(BF16), preferred_element_type=F32)
        hid = jnp.sin(f1_ref[...] * (hid + b1_ref[...]))
        hid = jnp.dot(hid.astype(BF16), w2_ref[...].astype(BF16), preferred_element_type=F32)
        hid_ref[...] = jnp.sin(f2_ref[...] * (hid + b2_ref[...])).astype(BF16)

    hid = hid_ref[...]
    win = jnp.exp(-t_ref[...] * delta_ref[...])

    def spectrum(w3_ref):
        h = jnp.dot(hid, w3_ref[...].astype(BF16), preferred_element_type=F32) * win
        h = h / (jnp.sum(jnp.abs(h), axis=0, keepdims=True) + EPS)
        return jnp.dot(fwd_ref[...], h.astype(BF16), preferred_element_type=F32)

    a = spectrum(w3f_ref)
    b = spectrum(w3b_ref)
    dc = lax.broadcasted_iota(jnp.int32, (L, 1), 0) == 0
    hc_ref[0] = a[:L] + b[:L]
    hs_ref[0] = a[L:] + jnp.where(dc, b[L:], -b[L:])


def _hy_filter(L, tc, fwd, w1, b1, f1, w2, b2, f2, w3):
    feats, t = _hy_feats(L)
    deltas = jnp.abs(jnp.linspace(math.log(HY_TARGET) / HY_FAST, math.log(HY_TARGET) / HY_SLOW,
                                  HY_W, dtype=F32))[None, :]
    w1p = jnp.pad(w1, ((0, LANES - HY_EMB), (0, 0)))
    nj = HY_W // tc
    full = lambda shape: pl.BlockSpec(shape, lambda o, j: (0,) * len(shape))
    out = jax.ShapeDtypeStruct((HY_ORDER, L, HY_W), F32)
    return pl.pallas_call(
        functools.partial(_hy_filter_kernel, L=L),
        grid=(HY_ORDER, nj),
        in_specs=[full((L, LANES)), full((L, 1)), full((LANES, HY_HID)), full((1, HY_HID)),
                  full((1, HY_HID)), full((HY_HID, HY_HID)), full((1, HY_HID)), full((1, HY_HID)),
                  pl.BlockSpec((HY_HID, tc), lambda o, j: (0, o * nj + j)),
                  pl.BlockSpec((HY_HID, tc), lambda o, j: (0, (HY_ORDER + o) * nj + j)),
                  pl.BlockSpec((1, tc), lambda o, j: (0, j)),
                  full((2 * L, L))],
        out_specs=[pl.BlockSpec((1, L, tc), lambda o, j: (o, 0, j)),
                   pl.BlockSpec((1, L, tc), lambda o, j: (o, 0, j))],
        out_shape=[out, out],
        scratch_shapes=[pltpu.VMEM((L, HY_HID), BF16)],
        compiler_params=_cparams(("arbitrary", "arbitrary")),
        name=f"hyena_filter_{L}",
    )(feats, t, w1p, b1.reshape(1, HY_HID), f1.reshape(1, HY_HID), w2, b2.reshape(1, HY_HID),
      f2.reshape(1, HY_HID), w3, w3, deltas, fwd)


def _hy_conv_kernel(v_ref, x1_ref, x2_ref, sv_ref, s1_ref, s2_ref, bias_ref, hc_ref, hs_ref,
                    fwd_ref, inv_ref, o_ref, *, L):
    row = lax.broadcasted_iota(jnp.int32, (L, 1), 0)
    first = row == 0
    last = row == L - 1

    def short_conv(u_ref, s_ref):
        u = u_ref[...]
        prev = jnp.where(first, 0.0, pltpu.roll(u, 1, 0))
        nxt = jnp.where(last, 0.0, pltpu.roll(u, L - 1, 0))
        return prev * s_ref[0:1, :] + u * s_ref[1:2, :] + nxt * s_ref[2:3, :]

    def long_conv(z, order):
        spec = jnp.dot(fwd_ref[...], z.astype(BF16), preferred_element_type=F32)
        zc, zs = spec[:L], spec[L:]
        hc, hs = hc_ref[order], hs_ref[order]
        yc = zc * hc - jnp.where(first, 0.0, zs * hs)
        ys = jnp.where(first, zs * hs, zc * hs + zs * hc)
        y = jnp.dot(inv_ref[...], jnp.concatenate([yc, ys], axis=0).astype(BF16),
                    preferred_element_type=F32)
        return y + z * bias_ref[order:order + 1, :]

    z = short_conv(x1_ref, s1_ref) * long_conv(short_conv(v_ref, sv_ref), 0)
    o_ref[...] = (short_conv(x2_ref, s2_ref) * long_conv(z, 1)).astype(BF16)


def _hy_conv(proj, row0, nseq, L, tc, hy_short, hy_bias, hc, hs, fwd, inv):
    nj = HY_W // tc
    rb = row0 // L
    cb = C_HY // tc
    col = lambda k: pl.BlockSpec((L, tc), lambda b, j: (rb + b, cb + k * nj + j))
    sw = lambda k: pl.BlockSpec((3, tc), lambda b, j: (0, k * nj + j))
    spec3 = pl.BlockSpec((HY_ORDER, L, tc), lambda b, j: (0, 0, j))
    return pl.pallas_call(
        functools.partial(_hy_conv_kernel, L=L),
        grid=(nseq, nj),
        in_specs=[col(0), col(1), col(2), sw(0), sw(1), sw(2),
                  pl.BlockSpec((HY_ORDER, tc), lambda b, j: (0, j)), spec3, spec3,
                  pl.BlockSpec((2 * L, L), lambda b, j: (0, 0)),
                  pl.BlockSpec((L, 2 * L), lambda b, j: (0, 0))],
        out_specs=pl.BlockSpec((L, tc), lambda b, j: (b, j)),
        out_shape=jax.ShapeDtypeStruct((nseq * L, HY_W), BF16),
        compiler_params=_cparams(("arbitrary", "arbitrary")),
        name=f"hyena_conv_{L}",
    )(proj, proj, proj, hy_short, hy_short, hy_short, hy_bias, hc, hs, fwd, inv)


def _rope_tables(L):
    n_rows = L // GRID_W
    pos_r = jnp.repeat(jnp.arange(n_rows, dtype=F32), GRID_W)
    pos_c = jnp.tile(jnp.arange(GRID_W, dtype=F32), n_rows)
    nf = RET_DK // 4
    inv = ROPE_BASE ** (-jnp.arange(nf, dtype=F32) / nf)
    ar = pos_r[:, None] * inv[None, :]
    ac = pos_c[:, None] * inv[None, :]
    cos = jnp.concatenate([jnp.cos(ar), jnp.cos(ar), jnp.cos(ac), jnp.cos(ac)], axis=-1)
    sin = jnp.concatenate([-jnp.sin(ar), jnp.sin(ar), -jnp.sin(ac), jnp.sin(ac)], axis=-1)
    return jnp.tile(cos, (1, 2)), jnp.tile(sin, (1, 2))


def _ret_kernel(*refs, L, latent):
    if latent:
        (q_ref, k_ref, v_ref, g_ref, dec_ref, cos_ref, sin_ref, s0_ref, o_ref,
         acc_ref, qs_ref, ks_ref) = refs
    else:
        q_ref, k_ref, v_ref, g_ref, dec_ref, o_ref, sn_ref, acc_ref, qs_ref, ks_ref = refs
    C = RET_CHUNK
    n = L // C
    lane = lax.broadcasted_iota(jnp.int32, (1, 2 * RET_DK), 1)
    q = q_ref[...]
    k = k_ref[...] * (RET_DK ** -0.5)
    if latent:
        nf = RET_DK // 4
        lo = (lane % (2 * nf)) < nf

        def rope(x):
            swapped = jnp.where(lo, pltpu.roll(x, 2 * RET_DK - nf, 1), pltpu.roll(x, nf, 1))
            return x * cos_ref[...] + swapped * sin_ref[...]

        q, k = rope(q), rope(k)
    qs_ref[...] = q
    ks_ref[...] = k
    ri = lax.broadcasted_iota(jnp.int32, (C, C), 0)
    ci = lax.broadcasted_iota(jnp.int32, (C, C), 1)
    rel = (ri - ci).astype(F32)
    rowf = ri.astype(F32)

    chains = [(h, d) for h in range(2) for d in range(2)]
    heads, vcols, dmats, q_decs, k_decs, c_decs, states = [], [], [], [], [], [], []
    for h, d in chains:
        dec = dec_ref[d, 2 * pl.program_id(1) + h]
        lg = jnp.minimum(dec, 0.0) - jnp.log1p(jnp.exp(-jnp.abs(dec)))
        if d == 0:
            dmats.append(jnp.where(rel >= 0, jnp.exp(rel * lg), 0.0))
            q_decs.append(jnp.exp((rowf + 1.0) * lg))
            k_decs.append(jnp.exp((C - 1.0 - rowf) * lg))
        else:
            dmats.append(jnp.where(rel <= 0, jnp.exp(-rel * lg), 0.0))
            q_decs.append(jnp.exp((C - rowf) * lg))
            k_decs.append(jnp.exp(rowf * lg))
        c_decs.append(jnp.exp(C * lg))
        heads.append((lane // RET_DK) == h)
        vcols.append(slice(h * RET_DV, (h + 1) * RET_DV))
        if latent:
            s_init = s0_ref[0, d, h]
            zero = jnp.zeros_like(s_init)
            states.append(jnp.concatenate([s_init, zero] if h == 0 else [zero, s_init], axis=0))
        else:
            states.append(jnp.zeros((2 * RET_DK, RET_DV), F32))

    nt = lambda a, b: lax.dot_general(a, b, (((1,), (1,)), ((), ())), preferred_element_type=F32)
    tn = lambda a, b: lax.dot_general(a, b, (((0,), (0,)), ((), ())), preferred_element_type=F32)
    nn = lambda a, b: jnp.dot(a, b, preferred_element_type=F32)
    for i in range(n):
        rows = [pl.ds((i if d == 0 else n - 1 - i) * C, C) for _, d in chains]
        qc = [jnp.where(hd, qs_ref[r, :], 0.0) for hd, r in zip(heads, rows)]
        kc = [ks_ref[r, :] for r in rows]
        vc = [v_ref[r, vc_].astype(BF16) for r, vc_ in zip(rows, vcols)]
        sc = [nt(q.astype(BF16), k.astype(BF16)) * dm for q, k, dm in zip(qc, kc, dmats)]
        qs = [nn((q * qd).astype(BF16), s.astype(BF16)) for q, qd, s in zip(qc, q_decs, states)]
        kv = [tn((k * kd).astype(BF16), v) for k, kd, v in zip(kc, k_decs, vc)]
        o = [nn(s_.astype(BF16), v) + q_ for s_, v, q_ in zip(sc, vc, qs)]
        for (h, d), r, vc_, o_ in zip(chains, rows, vcols, o):
            acc_ref[d, r, vc_] = o_
        states = [s * cd + k_ for s, cd, k_ in zip(states, c_decs, kv)]
    if not latent:
        for (h, d), s in zip(chains, states):
            sn_ref[0, d, h] = s[h * RET_DK:(h + 1) * RET_DK]

    for h in range(2):
        vcol = slice(h * RET_DV, (h + 1) * RET_DV)
        o = acc_ref[0, :, vcol] + acc_ref[1, :, vcol]
        o = o * lax.rsqrt(jnp.mean(o * o, axis=-1, keepdims=True) + EPS)
        g = g_ref[:, vcol]
        o_ref[:, vcol] = (o * (g * jax.nn.sigmoid(g))).astype(BF16)


def _retention(proj, row0, nseq, L, latent, dec, s0=None):
    rb = row0 // L
    npair = RET_HEADS // 2
    qk = lambda c0: pl.BlockSpec((L, 2 * RET_DK), lambda b, p: (rb + b, c0 // (2 * RET_DK) + p))
    vg = lambda c0: pl.BlockSpec((L, 2 * RET_DV), lambda b, p: (rb + b, c0 // (2 * RET_DV) + p))
    in_specs = [qk(C_RQ), qk(C_RK), vg(C_RV), vg(C_RG),
                pl.BlockSpec((2, RET_HEADS, 1, LANES), lambda b, p: (0, 0, 0, 0))]
    args = [proj, proj, proj, proj, dec]
    out_specs = [pl.BlockSpec((L, 2 * RET_DV), lambda b, p: (b, p))]
    out_shape = [jax.ShapeDtypeStruct((nseq * L, RET_W), BF16)]
    st_spec = pl.BlockSpec((1, 2, 2, RET_DK, RET_DV), lambda b, p: (b, 0, p, 0, 0))
    if latent:
        cos, sin = _rope_tables(L)
        tab = pl.BlockSpec((L, 2 * RET_DK), lambda b, p: (0, 0))
        in_specs += [tab, tab, st_spec]
        args += [cos, sin, s0]
    else:
        out_specs.append(st_spec)
        out_shape.append(jax.ShapeDtypeStruct((nseq, 2, RET_HEADS, RET_DK, RET_DV), F32))
    return pl.pallas_call(
        functools.partial(_ret_kernel, L=L, latent=latent),
        grid=(nseq, npair),
        in_specs=in_specs, out_specs=out_specs, out_shape=out_shape,
        scratch_shapes=[pltpu.VMEM((2, L, 2 * RET_DV), F32),
                        pltpu.VMEM((L, 2 * RET_DK), F32),
                        pltpu.VMEM((L, 2 * RET_DK), F32)],
        compiler_params=_cparams(("arbitrary", "arbitrary")),
        name=f"retention_{L}",
    )(*args)


DN_C = 128
DN_HP_CTX = 4
DN_HP_LAT = 2
DN_GROUP = 2


def _dn_gate_params(a_log, dt_bias):
    gp = jnp.stack([a_log.reshape(-1), dt_bias.reshape(-1)], axis=0)
    return jnp.pad(gp, ((0, 0), (0, LANES - 2 * DN_HEADS)))


def _dot16(a, b):
    return jnp.dot(a.astype(BF16), b.astype(BF16), preferred_element_type=F32)


def _dot16_nt(a, b):
    return lax.dot_general(a.astype(BF16), b.astype(BF16), (((1,), (1,)), ((), ())),
                           preferred_element_type=F32)


def _dot16_tn(a, b):
    return lax.dot_general(a.astype(BF16), b.astype(BF16), (((0,), (0,)), ((), ())),
                           preferred_element_type=F32)


def _split3(x):
    p1 = x.astype(BF16)
    r = x - p1.astype(F32)
    p2 = r.astype(BF16)
    return p1, p2, (r - p2.astype(F32)).astype(BF16)


def _dot3(parts, rhs16):
    return sum(jnp.dot(p, rhs16, preferred_element_type=F32) for p in parts)


def _dot3_rhs(lhs16, parts):
    return sum(jnp.dot(lhs16, p, preferred_element_type=F32) for p in parts)


def _block_rows(c, size):
    if isinstance(c, int):
        return pl.ds(c * size, size)
    return pl.ds(pl.multiple_of(c * size, size), size)


def _dn_kernel(*refs, L, latent, hp):
    (q_ref, k_ref, v_ref, z_ref, ab_ref, cq_ref, ck_ref, cv_ref, gp_ref, nw_ref) = refs[:10]
    if latent:
        s0_ref, o_ref = refs[10:12]
    else:
        o_ref, sn_ref = refs[10:12]
    (qs_ref, ks_ref, vs_ref, beta_ref, g_ref, u_ref, acc_ref,
     w_ref, at_ref, qd_ref, kdt_ref, egl_ref) = refs[12:]
    C = DN_C
    n = L // C
    HP = hp
    row = lax.broadcasted_iota(jnp.int32, (L, 1), 0)
    first = row == 0
    last = row == L - 1

    def conv_silu(u_ref, c_ref, cols):
        u = u_ref[:, cols]
        prev = jnp.where(first, 0.0, pltpu.roll(u, 1, 0))
        nxt = jnp.where(last, 0.0, pltpu.roll(u, L - 1, 0))
        y = prev * c_ref[0:1, cols] + u * c_ref[1:2, cols] + nxt * c_ref[2:3, cols]
        return y * jax.nn.sigmoid(y)

    def l2n(x):
        return x * lax.rsqrt(jnp.sum(x * x, axis=-1, keepdims=True) + EPS)

    for hh in range(HP):
        cols = slice(hh * DN_DK, (hh + 1) * DN_DK)
        qs_ref[hh] = l2n(conv_silu(q_ref, cq_ref, cols)) * (DN_DK ** -0.5)
        ks_ref[hh] = l2n(conv_silu(k_ref, ck_ref, cols))
        vs_ref[hh] = conv_silu(v_ref, cv_ref, cols)

    ri = lax.broadcasted_iota(jnp.int32, (C, C), 0)
    ci = lax.broadcasted_iota(jnp.int32, (C, C), 1)
    rx = ri ^ ci
    eye = (ri == ci).astype(F32)

    ab = ab_ref[...]
    lane = lax.broadcasted_iota(jnp.int32, (1, LANES), 1)
    x = ab + gp_ref[1:2, :]
    g_all = -jnp.exp(gp_ref[0:1, :]) * (jnp.maximum(x, 0.0) + jnp.log1p(jnp.exp(-jnp.abs(x))))
    gates3 = _split3(jnp.where(lane < 2 * DN_HEADS, g_all, jax.nn.sigmoid(ab)))
    for hh in range(HP):
        head = HP * pl.program_id(1) + hh
        for d in range(2):
            sel_g = (ri == d * DN_HEADS + head).astype(BF16)
            sel_b = (ri == (2 + d) * DN_HEADS + head).astype(BF16)
            g_ref[2 * hh + d] = _dot3(gates3, sel_g)
            beta_ref[2 * hh + d] = _dot3(gates3[:2], sel_b)

    incl = [ri >= ci, ri <= ci]
    strict = [ri > ci, ri < ci]
    half_bit = [ri, ci]

    def quadrant(d, b):
        return (rx >= b // 2) & (rx < b) & ((half_bit[d] & (b // 2)) != 0)

    def prepare(insts):
        each = lambda fn, *cols: [fn(*a) for a in zip(*cols)]
        hs = [hh for hh, _, _ in insts]
        ds = [d for _, _, d in insts]
        hds = [2 * hh + d for hh, _, d in insts]
        rows = [_block_rows(c, C) for _, c, _ in insts]
        kc = each(lambda hh, r: ks_ref[hh, r, :], hs, rows)
        qc = each(lambda hh, r: qs_ref[hh, r, :], hs, rows)
        beta = each(lambda hd, r: beta_ref[hd, r, :], hds, rows)
        gc = each(lambda d, hd, r: _dot3_rhs(incl[d].astype(BF16), _split3(g_ref[hd, r, :])),
                  ds, hds, rows)
        decay = each(lambda d, g: jnp.where(incl[d], jnp.exp(jnp.where(incl[d], g - g.T, 0.0)), 0.0),
                     ds, gc)
        kb = each(lambda k, b_: k * b_, kc, beta)
        kk = each(_dot16_nt, kb, kc)
        a_mat = each(lambda d, x, dc: jnp.where(strict[d], x * dc, 0.0), ds, kk, decay)
        inv = each(lambda d, a: eye - jnp.where(quadrant(d, 2), a, 0.0), ds, a_mat)
        b = 4
        while b <= C:
            off = each(lambda d, a: jnp.where(quadrant(d, b), a, 0.0), ds, a_mat)
            t = each(_dot16, inv, off)
            t = each(_dot16, t, inv)
            inv = each(lambda x, y: x - y, inv, t)
            b *= 2
        egc = [jnp.exp(g) for g in gc]
        rhs = each(lambda hh, r, b_, k, e: jnp.concatenate([vs_ref[hh, r, :] * b_, k * e], axis=1),
                   hs, rows, beta, kb, egc)
        sol = each(_dot16, inv, rhs)
        qk = each(_dot16_nt, qc, kc)
        for (_, c, d), hd, r, s, a, dc, g, e, q, k in zip(insts, hds, rows, sol, qk, decay, gc, egc,
                                                         qc, kc):
            g_last = g[C - 1:C, :] if d == 0 else g[0:1, :]
            u_ref[hd, r, :] = s[:, :DN_DV]
            w_ref[hd, r, :] = s[:, DN_DV:].astype(BF16)
            at_ref[hd, r, :] = (a * dc).astype(BF16)
            qd_ref[hd, r, :] = (q * e).astype(BF16)
            kdt_ref[hd, r, :] = (k * jnp.exp(g_last - g)).T.astype(BF16)
            egl_ref[hd, _block_rows(c, 8), :] = jnp.broadcast_to(jnp.exp(g_last), (8, LANES))

    group = min(n, DN_GROUP)

    def prepare_group(j, carry):
        for h0 in range(0, HP, 2):
            prepare([(hh, group * j + t, d) for hh in (h0, h0 + 1) for t in range(group)
                     for d in range(2)])
        return carry

    if n == group:
        prepare_group(0, 0)
    else:
        lax.fori_loop(0, n // group, prepare_group, 0)

    chains = [(hh, d) for hh in range(HP) for d in range(2)]

    def advance(i, states):
        cs = [i if d == 0 else n - 1 - i for _, d in chains]
        rows = [_block_rows(c, C) for c in cs]
        hds = [2 * hh + d for hh, d in chains]
        dot = lambda a, b: jnp.dot(a, b, preferred_element_type=F32)
        s16 = [s.astype(BF16) for s in states]
        ws = [dot(w_ref[hd, r, :], s) for hd, r, s in zip(hds, rows, s16)]
        qs = [dot(qd_ref[hd, r, :], s) for hd, r, s in zip(hds, rows, s16)]
        v16 = [(u_ref[hd, r, :] - w).astype(BF16) for hd, r, w in zip(hds, rows, ws)]
        av = [dot(at_ref[hd, r, :], v) for hd, r, v in zip(hds, rows, v16)]
        kv = [dot(kdt_ref[hd, r, :], v) for hd, r, v in zip(hds, rows, v16)]
        new_states = []
        for hd, r, c, s, q, a, k in zip(hds, rows, cs, states, qs, av, kv):
            acc_ref[hd, r, :] = q + a
            new_states.append(s * egl_ref[hd, _block_rows(c, 8), :][0:1, :] + k)
        return tuple(new_states)

    if latent:
        init = tuple(s0_ref[0, d, hh] for hh, d in chains)
    else:
        init = (jnp.zeros((DN_DK, DN_DV), F32),) * len(chains)
    if n == 2:
        fin = advance(1, advance(0, init))
    else:
        fin = lax.fori_loop(0, n, advance, init)
    if not latent:
        for (hh, d), s in zip(chains, fin):
            sn_ref[0, d, hh] = s

    for hh in range(HP):
        cols = slice(hh * DN_DV, (hh + 1) * DN_DV)
        o = acc_ref[2 * hh] + acc_ref[2 * hh + 1]
        o = o * lax.rsqrt(jnp.mean(o * o, axis=-1, keepdims=True) + EPS) * nw_ref[...]
        z = z_ref[:, cols]
        o_ref[:, cols] = (o * (z * jax.nn.sigmoid(z))).astype(BF16)


def _deltanet(proj, ab, row0, nseq, L, latent, hp, dn_conv, gp, dn_norm, s0=None):
    rb = row0 // L
    wide = hp * DN_DK
    cb = C_DQKV // wide
    ng = DN_HEADS // hp
    col = lambda k: pl.BlockSpec((L, wide), lambda b, h: (rb + b, cb + k * ng + h))
    cw = lambda k: pl.BlockSpec((3, wide), lambda b, h: (0, k * ng + h))
    in_specs = [col(0), col(1), col(2),
                pl.BlockSpec((L, wide), lambda b, h: (rb + b, C_DZ // wide + h)),
                pl.BlockSpec((L, LANES), lambda b, h: (rb + b, 0)),
                cw(0), cw(1), cw(2),
                pl.BlockSpec((2, LANES), lambda b, h: (0, 0)),
                pl.BlockSpec((1, DN_DV), lambda b, h: (0, 0))]
    args = [proj, proj, proj, proj, ab, dn_conv, dn_conv, dn_conv, gp, dn_norm.reshape(1, DN_DV)]
    out_specs = [pl.BlockSpec((L, wide), lambda b, h: (b, h))]
    out_shape = [jax.ShapeDtypeStruct((nseq * L, DN_W), BF16)]
    st_spec = pl.BlockSpec((1, 2, hp, DN_DK, DN_DV), lambda b, h: (b, 0, h, 0, 0))
    if latent:
        in_specs.append(st_spec)
        args.append(s0)
    else:
        out_specs.append(st_spec)
        out_shape.append(jax.ShapeDtypeStruct((nseq, 2, DN_HEADS, DN_DK, DN_DV), F32))
    nslot = 2 * hp
    return pl.pallas_call(
        functools.partial(_dn_kernel, L=L, latent=latent, hp=hp),
        grid=(nseq, ng),
        in_specs=in_specs, out_specs=out_specs, out_shape=out_shape,
        scratch_shapes=([pltpu.VMEM((hp, L, DN_DK), F32)] * 3
                        + [pltpu.VMEM((nslot, L, DN_DV), F32)] * 4
                        + [pltpu.VMEM((nslot, L, DN_DV), BF16)] * 4
                        + [pltpu.VMEM((nslot, 8 * (L // DN_C), LANES), F32)]),
        compiler_params=_cparams(("arbitrary", "arbitrary")),
        name=f"deltanet_{L}",
    )(*args)


def _prep_weights(w_in, p_ret, p_hy, p_dn, w_o, w_up, ffn_conv, w_down):
    w_a = w_in[:, :, :O_DA].astype(BF16)
    w_g = w_in[:, :, O_MG:].astype(BF16)
    w_ab = jnp.pad(w_in[:, :, O_DA:O_MG].astype(BF16), ((0, 0), (0, 0), (0, LANES - (O_MG - O_DA))))
    padc = ((0, 0), (0, 0), (0, FFN_PAD - FFN_DIM))
    wu_gate = jnp.pad(w_up[:, :, :FFN_DIM].astype(BF16), padc)
    wu_val = jnp.pad(w_up[:, :, FFN_DIM:].astype(BF16), padc)
    cw = jnp.concatenate([jnp.pad(ffn_conv[:, :, :FFN_DIM], padc),
                          jnp.pad(ffn_conv[:, :, FFN_DIM:], padc)], axis=2)
    wd = jnp.pad(w_down.astype(BF16), ((0, 0), (0, FFN_PAD - FFN_DIM), (0, 0)))
    return (w_a, w_g, w_ab, p_ret.astype(BF16), p_hy.astype(BF16), p_dn.astype(BF16),
            w_o.astype(BF16), wu_gate, wu_val, cw, wd)


def kernel(x_prompt, x_sample, state_ret, state_dn, c, c_ctx, w_ada, b_ada, norm1, w_in, ret_decay, hy_short, hy_w1, hy_b1, hy_freq1, hy_w2, hy_b2, hy_freq2, hy_w3, hy_bias, dn_conv, dn_a_log, dn_dt_bias, dn_norm, p_ret, p_hy, p_dn, w_o, norm2, w_up, ffn_conv, w_down, norm_f):
    x = jnp.concatenate([x_prompt.reshape(MP, D_MODEL), x_sample.reshape(MS, D_MODEL)], axis=0)
    cond = jnp.concatenate([c_ctx[None], c, jnp.zeros((3, D_MODEL), F32)], axis=0)
    mod_all = _adaln(cond, w_ada, b_ada)
    new_ret, new_dn = [], []
    dft_p, dft_s = _dft_mats(SEQ), _dft_mats(DEC_SEQ)
    (w_a, w_g, w_ab, pr, ph, pd, wo, wu_gate, wu_val, cw, wd) = _prep_weights(
        w_in, p_ret, p_hy, p_dn, w_o, w_up, ffn_conv, w_down)
    for l in range(DEPTH):
        mod = mod_all[l].reshape(8, 1, N_MOD * D_MODEL)
        proj, ab = _proj_in(x, norm1[l].reshape(1, D_MODEL), mod, l, w_a, w_g, w_ab)

        dec = jnp.broadcast_to(ret_decay[l][:, :, None, None], (2, RET_HEADS, 1, LANES))
        rp, s_r = _retention(proj, 0, BATCH, SEQ, False, dec)
        (rs,) = _retention(proj, MP, DEC_BATCH, DEC_SEQ, True, dec, state_ret[:, l])
        filt = (hy_w1[l], hy_b1[l], hy_freq1[l], hy_w2[l], hy_b2[l], hy_freq2[l], hy_w3[l])
        hcp, hsp = _hy_filter(SEQ, HY_TC_P, dft_p[0], *filt)
        hcs, hss = _hy_filter(DEC_SEQ, HY_TC_S, dft_s[0], *filt)
        hp = _hy_conv(proj, 0, BATCH, SEQ, HY_TC_P, hy_short[l], hy_bias[l], hcp, hsp, *dft_p)
        hs = _hy_conv(proj, MP, DEC_BATCH, DEC_SEQ, HY_TC_S, hy_short[l], hy_bias[l], hcs, hss, *dft_s)
        gp = _dn_gate_params(dn_a_log[l], dn_dt_bias[l])
        dp, s_d = _deltanet(proj, ab, 0, BATCH, SEQ, False, DN_HP_CTX, dn_conv[l], gp, dn_norm[l])
        (ds,) = _deltanet(proj, ab, MP, DEC_BATCH, DEC_SEQ, True, DN_HP_LAT, dn_conv[l], gp, dn_norm[l],
                          state_dn[:, l])
        new_ret.append(s_r)
        new_dn.append(s_d)
        mix = _merge((rp, hp, dp), (rs, hs, ds), l, pr, ph, pd, proj)
        x = _resid_matmul(mix, l, wo, x, mod, 2, TN_WIDE, "attn_out")
        act = _ffn_up(x, norm2[l].reshape(1, D_MODEL), mod, l, wu_gate, wu_val, cw)
        x = _resid_matmul(act, l, wd, x, mod, 5, TN, "ffn_down")
    nf = norm_f.reshape(1, D_MODEL)
    y_prompt = _final_norm(x, nf, 0, MP).reshape(BATCH, SEQ, D_MODEL)
    y_sample = _final_norm(x, nf, MP, MS).reshape(DEC_BATCH, DEC_SEQ, D_MODEL)
    return (y_prompt, y_sample, jnp.stack(new_ret, axis=1), jnp.stack(new_dn, axis=1))
```

```python
import functools
import math

import jax
import jax.numpy as jnp
import numpy as np
from jax import lax
from jax.experimental import pallas as pl
from jax.experimental.pallas import tpu as pltpu

F32 = jnp.float32
BF16 = jnp.bfloat16

D_MODEL = 2048
BATCH = 32
SEQ = 256
DEPTH = 2
DEC_BATCH = 4
DEC_SEQ = 1024
GRID_W = 64
EPS = 1e-6
ROPE_BASE = 10000.0
RET_HEADS = 8
RET_DK = 64
RET_DV = 128
RET_QK = RET_HEADS * RET_DK
RET_W = RET_HEADS * RET_DV
RET_CHUNK = 128
HY_W = 1024
HY_ORDER = 2
HY_EMB = 33
HY_HID = 64
HY_TARGET = 1e-2
HY_FAST = 0.3
HY_SLOW = 1.5
DN_HEADS = 8
DN_DK = 128
DN_DV = 128
DN_QK = DN_HEADS * DN_DK
DN_W = DN_HEADS * DN_DV
DN_CHUNK = 64
N_BRANCH = 3
FFN_DIM = 5504
N_MOD = 6

MP = BATCH * SEQ
MS = DEC_BATCH * DEC_SEQ
M_TOK = MP + MS

LANES = 128
TM = 1024
TN = 512
TN_WIDE = 1024
FFN_PAD = 5632
HY_TC_P = 1024
HY_TC_S = 256
VMEM_LIMIT = 56 * 1024 * 1024

C_RQ, C_RK, C_RV, C_RG = 0, 512, 1024, 2048
C_HY, C_DQKV, C_DZ, C_MG = 3072, 6144, 9216, 10240
N_MAIN = 16384
O_DA, O_MG = 10240, 10272


def _cparams(sem):
    return pltpu.CompilerParams(dimension_semantics=sem, vmem_limit_bytes=VMEM_LIMIT)


def _mod_row(i):
    return jnp.where(i < MP // TM, 0, 1 + (i - MP // TM) // (DEC_SEQ // TM))


def _mod_kernel(cond_ref, w_ref, b_ref, o_ref):
    c = cond_ref[...]
    s = (c * jax.nn.sigmoid(c)).astype(BF16)
    o_ref[0] = jnp.dot(s, w_ref[0].astype(BF16), preferred_element_type=F32) + b_ref[0]


def _adaln(cond, w_ada, b_ada):
    tn = 1024
    return pl.pallas_call(
        _mod_kernel,
        grid=(DEPTH, N_MOD * D_MODEL // tn),
        in_specs=[pl.BlockSpec((8, D_MODEL), lambda l, j: (0, 0)),
                  pl.BlockSpec((1, D_MODEL, tn), lambda l, j: (l, 0, j)),
                  pl.BlockSpec((1, 1, tn), lambda l, j: (l, 0, j))],
        out_specs=pl.BlockSpec((1, 8, tn), lambda l, j: (l, 0, j)),
        out_shape=jax.ShapeDtypeStruct((DEPTH, 8, N_MOD * D_MODEL), F32),
        compiler_params=_cparams(("arbitrary", "arbitrary")),
        name="adaln_mod",
    )(cond, w_ada, b_ada.reshape(DEPTH, 1, N_MOD * D_MODEL))


def _modulated_norm(x, nw, sc, sh):
    var = jnp.mean(x * x, axis=-1, keepdims=True)
    return (x * lax.rsqrt(var + EPS) * nw) * (1.0 + sc) + sh


def _proj_in_kernel(x_ref, nw_ref, sh_ref, sc_ref, wa_ref, wg_ref, wab_ref, o_ref, ab_ref, h_ref):
    j = pl.program_id(1)

    @pl.when(j == 0)
    def _():
        h = _modulated_norm(x_ref[...], nw_ref[...], sc_ref[0], sh_ref[0]).astype(BF16)
        h_ref[...] = h
        ab_ref[...] = jnp.dot(h, wab_ref[...], preferred_element_type=F32)

    @pl.when(j < NT_A)
    def _():
        o_ref[...] = jnp.dot(h_ref[...], wa_ref[...], preferred_element_type=F32)

    @pl.when(j >= NT_A)
    def _():
        o_ref[...] = jnp.dot(h_ref[...], wg_ref[...], preferred_element_type=F32)


NT_A = C_MG // TN_WIDE


def _proj_in(x, nw, mod, l, w_a, w_g, w_ab):
    return pl.pallas_call(
        _proj_in_kernel,
        grid=(M_TOK // TM, N_MAIN // TN_WIDE),
        in_specs=[pl.BlockSpec((TM, D_MODEL), lambda i, j: (i, 0)),
                  pl.BlockSpec((1, D_MODEL), lambda i, j: (0, 0)),
                  pl.BlockSpec((1, 1, D_MODEL), lambda i, j: (_mod_row(i), 0, 0)),
                  pl.BlockSpec((1, 1, D_MODEL), lambda i, j: (_mod_row(i), 0, 1)),
                  pl.BlockSpec((None, None, D_MODEL, TN_WIDE),
                               lambda i, j: (jnp.minimum(j, NT_A - 1), l, 0, 0)),
                  pl.BlockSpec((None, D_MODEL, TN_WIDE), lambda i, j: (l, 0, jnp.maximum(j - NT_A, 0))),
                  pl.BlockSpec((None, D_MODEL, LANES), lambda i, j: (l, 0, 0))],
        out_specs=[pl.BlockSpec((TM, TN_WIDE), lambda i, j: (i, j)),
                   pl.BlockSpec((TM, LANES), lambda i, j: (i, 0))],
        out_shape=[jax.ShapeDtypeStruct((M_TOK, N_MAIN), F32),
                   jax.ShapeDtypeStruct((M_TOK, LANES), F32)],
        scratch_shapes=[pltpu.VMEM((TM, D_MODEL), BF16)],
        compiler_params=_cparams(("arbitrary", "arbitrary")),
        name="proj_in",
    )(x, nw, mod, mod, w_a, w_g, w_ab)


def _merge_kernel(rp_ref, hp_ref, dp_ref, rs_ref, hs_ref, ds_ref, pr_ref, ph_ref, pd_ref,
                  gr_ref, gh_ref, gd_ref, o_ref):
    def mix(r_ref, h_ref, d_ref):
        m = jax.nn.sigmoid(gr_ref[...]) * jnp.dot(r_ref[...], pr_ref[...], preferred_element_type=F32)
        m += jax.nn.sigmoid(gh_ref[...]) * jnp.dot(h_ref[...], ph_ref[...], preferred_element_type=F32)
        m += jax.nn.sigmoid(gd_ref[...]) * jnp.dot(d_ref[...], pd_ref[...], preferred_element_type=F32)
        o_ref[...] = m.astype(BF16)

    is_context = pl.program_id(0) < MP // TM

    @pl.when(is_context)
    def _():
        mix(rp_ref, hp_ref, dp_ref)

    @pl.when(jnp.logical_not(is_context))
    def _():
        mix(rs_ref, hs_ref, ds_ref)


def _merge(ctx, lat, l, p_ret, p_hy, p_dn, proj):
    g0 = C_MG // TN
    gstep = D_MODEL // TN
    npt = MP // TM
    cspec = pl.BlockSpec((TM, RET_W), lambda i, j: (jnp.minimum(i, npt - 1), 0))
    lspec = pl.BlockSpec((TM, RET_W), lambda i, j: (jnp.maximum(i - npt, 0), 0))
    pspec = pl.BlockSpec((None, RET_W, TN), lambda i, j: (l, 0, j))
    return pl.pallas_call(
        _merge_kernel,
        grid=(M_TOK // TM, D_MODEL // TN),
        in_specs=[cspec, cspec, cspec, lspec, lspec, lspec, pspec, pspec, pspec,
                  pl.BlockSpec((TM, TN), lambda i, j: (i, g0 + j)),
                  pl.BlockSpec((TM, TN), lambda i, j: (i, g0 + gstep + j)),
                  pl.BlockSpec((TM, TN), lambda i, j: (i, g0 + 2 * gstep + j))],
        out_specs=pl.BlockSpec((TM, TN), lambda i, j: (i, j)),
        out_shape=jax.ShapeDtypeStruct((M_TOK, D_MODEL), BF16),
        compiler_params=_cparams(("arbitrary", "arbitrary")),
        name="branch_merge",
    )(*ctx, *lat, p_ret, p_hy, p_dn, proj, proj, proj)


def _resid_kernel(a_ref, w_ref, x_ref, g_ref, o_ref):
    o_ref[...] = x_ref[...] + g_ref[0] * jnp.dot(a_ref[...], w_ref[...], preferred_element_type=F32)


def _resid_matmul(a, l, w, x, mod, gate_idx, tn, name):
    k = a.shape[1]
    gb = gate_idx * (D_MODEL // tn)
    return pl.pallas_call(
        _resid_kernel,
        grid=(M_TOK // TM, D_MODEL // tn),
        in_specs=[pl.BlockSpec((TM, k), lambda i, j: (i, 0)),
                  pl.BlockSpec((None, k, tn), lambda i, j: (l, 0, j)),
                  pl.BlockSpec((TM, tn), lambda i, j: (i, j)),
                  pl.BlockSpec((1, 1, tn), lambda i, j: (_mod_row(i), 0, gb + j))],
        out_specs=pl.BlockSpec((TM, tn), lambda i, j: (i, j)),
        out_shape=jax.ShapeDtypeStruct((M_TOK, D_MODEL), F32),
        compiler_params=_cparams(("arbitrary", "arbitrary")),
        name=name,
    )(a, w, x, mod)


PAD_ROWS = 8


def _ffn_up_kernel(x_ref, nw_ref, sh_ref, sc_ref, wa_ref, wb_ref, ca_ref, cb_ref, o_ref,
                   h_ref, ua_ref, ub_ref):
    i = pl.program_id(0)

    @pl.when(pl.program_id(1) == 0)
    def _():
        h_ref[...] = _modulated_norm(x_ref[...], nw_ref[...], sc_ref[0], sh_ref[0]).astype(BF16)
        zeros = jnp.zeros((PAD_ROWS, TN), F32)
        for ref in (ua_ref, ub_ref):
            ref[pl.ds(0, PAD_ROWS), :] = zeros
            ref[pl.ds(PAD_ROWS + TM, PAD_ROWS), :] = zeros

    h = h_ref[...]
    ua_ref[pl.ds(PAD_ROWS, TM), :] = jnp.dot(h, wa_ref[...], preferred_element_type=F32)
    ub_ref[pl.ds(PAD_ROWS, TM), :] = jnp.dot(h, wb_ref[...], preferred_element_type=F32)

    seq = jnp.where(i < MP // TM, SEQ, DEC_SEQ)
    pos = lax.broadcasted_iota(jnp.int32, (TM, 1), 0) & (seq - 1)
    has_prev = pos != 0
    has_next = pos != seq - 1

    def conv(u_ref, c_ref):
        prev = jnp.where(has_prev, u_ref[pl.ds(PAD_ROWS - 1, TM), :], 0.0)
        nxt = jnp.where(has_next, u_ref[pl.ds(PAD_ROWS + 1, TM), :], 0.0)
        mid = u_ref[pl.ds(PAD_ROWS, TM), :]
        return prev * c_ref[0:1, :] + mid * c_ref[1:2, :] + nxt * c_ref[2:3, :]

    ga = conv(ua_ref, ca_ref)
    gb = conv(ub_ref, cb_ref)
    o_ref[...] = (ga * jax.nn.sigmoid(ga) * gb).astype(BF16)


def _ffn_up(x, nw, mod, l, w_gate, w_val, conv_w):
    nj = FFN_PAD // TN
    return pl.pallas_call(
        _ffn_up_kernel,
        grid=(M_TOK // TM, nj),
        in_specs=[pl.BlockSpec((TM, D_MODEL), lambda i, j: (i, 0)),
                  pl.BlockSpec((1, D_MODEL), lambda i, j: (0, 0)),
                  pl.BlockSpec((1, 1, D_MODEL), lambda i, j: (_mod_row(i), 0, 3)),
                  pl.BlockSpec((1, 1, D_MODEL), lambda i, j: (_mod_row(i), 0, 4)),
                  pl.BlockSpec((None, None, D_MODEL, TN), lambda i, j: (0, l, 0, j)),
                  pl.BlockSpec((None, None, D_MODEL, TN), lambda i, j: (1, l, 0, j)),
                  pl.BlockSpec((None, 3, TN), lambda i, j: (l, 0, j)),
                  pl.BlockSpec((None, 3, TN), lambda i, j: (l, 0, nj + j))],
        out_specs=pl.BlockSpec((TM, TN), lambda i, j: (i, j)),
        out_shape=jax.ShapeDtypeStruct((M_TOK, FFN_PAD), BF16),
        scratch_shapes=[pltpu.VMEM((TM, D_MODEL), BF16),
                        pltpu.VMEM((TM + 2 * PAD_ROWS, TN), F32),
                        pltpu.VMEM((TM + 2 * PAD_ROWS, TN), F32)],
        compiler_params=_cparams(("arbitrary", "arbitrary")),
        name="ffn_up",
    )(x, nw, mod, mod, w_gate, w_val, conv_w, conv_w)


def _final_norm_kernel(x_ref, w_ref, o_ref):
    x = x_ref[...]
    var = jnp.mean(x * x, axis=-1, keepdims=True)
    o_ref[...] = x * lax.rsqrt(var + EPS) * w_ref[...]


def _final_norm(x, w, row0, rows):
    rb = row0 // TM
    return pl.pallas_call(
        _final_norm_kernel,
        grid=(rows // TM,),
        in_specs=[pl.BlockSpec((TM, D_MODEL), lambda i: (rb + i, 0)),
                  pl.BlockSpec((1, D_MODEL), lambda i: (0, 0))],
        out_specs=pl.BlockSpec((TM, D_MODEL), lambda i: (i, 0)),
        out_shape=jax.ShapeDtypeStruct((rows, D_MODEL), F32),
        compiler_params=_cparams(("arbitrary",)),
        name=f"final_norm_{row0}",
    )(x, w)


def _dft_mats(L):
    n = 2 * L
    t = np.arange(L)
    ang = 2.0 * np.pi * ((t[:, None] * t[None, :]) % n) / n
    fc, fs = np.cos(ang), np.sin(ang)
    alt = (-1.0) ** t
    fs[0, :] = alt
    fwd = np.concatenate([fc, fs], axis=0)
    w = np.full((L, 1), 2.0)
    w[0] = 1.0
    gc = (fc * w / n).T
    gs = (fs * 2.0 / n).T
    gs[:, 0] = alt / n
    inv = np.concatenate([gc, gs], axis=1)
    return jnp.asarray(fwd, BF16), jnp.asarray(inv, BF16)


def _hy_feats(L):
    t = jnp.linspace(0.0, 1.0, L, dtype=F32)[:, None]
    bands = (HY_EMB - 1) // 2
    wpos = 2.0 * math.pi * jnp.arange(L, dtype=F32)[:, None] / L
    fr = jnp.linspace(1e-4, bands - 1, bands, dtype=F32)[None, :]
    feats = jnp.concatenate([t, jnp.cos(fr * wpos), -jnp.sin(fr * wpos)], axis=-1)
    return jnp.pad(feats, ((0, 0), (0, LANES - HY_EMB))), t


def _hy_filter_kernel(feats_ref, t_ref, w1_ref, b1_ref, f1_ref, w2_ref, b2_ref, f2_ref,
                      w3f_ref, w3b_ref, delta_ref, fwd_ref, hc_ref, hs_ref, hid_ref, *, L):
    @pl.when((pl.program_id(0) == 0) & (pl.program_id(1) == 0))
    def _():
        hid = jnp.dot(feats_ref[...].astype(BF16), w1_ref[...].astype(BF16), preferred_element_type=F32)
        hid = jnp.sin(f1_ref[...] * (hid + b1_ref[...]))
        hid = jnp.dot(hid.astype(BF16), w2_ref[...].astype(BF16), preferred_element_type=F32)
        hid_ref[...] = jnp.sin(f2_ref[...] * (hid + b2_ref[...])).astype(BF16)

    hid = hid_ref[...]
    win = jnp.exp(-t_ref[...] * delta_ref[...])

    def spectrum(w3_ref):
        h = jnp.dot(hid, w3_ref[...].astype(BF16), preferred_element_type=F32) * win
        h = h / (jnp.sum(jnp.abs(h), axis=0, keepdims=True) + EPS)
        return jnp.dot(fwd_ref[...], h.astype(BF16), preferred_element_type=F32)

    a = spectrum(w3f_ref)
    b = spectrum(w3b_ref)
    dc = lax.broadcasted_iota(jnp.int32, (L, 1), 0) == 0
    hc_ref[0] = a[:L] + b[:L]
    hs_ref[0] = a[L:] + jnp.where(dc, b[L:], -b[L:])


def _hy_filter(L, tc, fwd, w1, b1, f1, w2, b2, f2, w3):
    feats, t = _hy_feats(L)
    deltas = jnp.abs(jnp.linspace(math.log(HY_TARGET) / HY_FAST, math.log(HY_TARGET) / HY_SLOW,
                                  HY_W, dtype=F32))[None, :]
    w1p = jnp.pad(w1, ((0, LANES - HY_EMB), (0, 0)))
    nj = HY_W // tc
    full = lambda shape: pl.BlockSpec(shape, lambda o, j: (0,) * len(shape))
    out = jax.ShapeDtypeStruct((HY_ORDER, L, HY_W), F32)
    return pl.pallas_call(
        functools.partial(_hy_filter_kernel, L=L),
        grid=(HY_ORDER, nj),
        in_specs=[full((L, LANES)), full((L, 1)), full((LANES, HY_HID)), full((1, HY_HID)),
                  full((1, HY_HID)), full((HY_HID, HY_HID)), full((1, HY_HID)), full((1, HY_HID)),
                  pl.BlockSpec((HY_HID, tc), lambda o, j: (0, o * nj + j)),
                  pl.BlockSpec((HY_HID, tc), lambda o, j: (0, (HY_ORDER + o) * nj + j)),
                  pl.BlockSpec((1, tc), lambda o, j: (0, j)),
                  full((2 * L, L))],
        out_specs=[pl.BlockSpec((1, L, tc), lambda o, j: (o, 0, j)),
                   pl.BlockSpec((1, L, tc), lambda o, j: (o, 0, j))],
        out_shape=[out, out],
        scratch_shapes=[pltpu.VMEM((L, HY_HID), BF16)],
        compiler_params=_cparams(("arbitrary", "arbitrary")),
        name=f"hyena_filter_{L}",
    )(feats, t, w1p, b1.reshape(1, HY_HID), f1.reshape(1, HY_HID), w2, b2.reshape(1, HY_HID),
      f2.reshape(1, HY_HID), w3, w3, deltas, fwd)


def _hy_conv_kernel(v_ref, x1_ref, x2_ref, sv_ref, s1_ref, s2_ref, bias_ref, hc_ref, hs_ref,
                    fwd_ref, inv_ref, o_ref, *, L):
    row = lax.broadcasted_iota(jnp.int32, (L, 1), 0)
    first = row == 0
    last = row == L - 1

    def short_conv(u_ref, s_ref):
        u = u_ref[...]
        prev = jnp.where(first, 0.0, pltpu.roll(u, 1, 0))
        nxt = jnp.where(last, 0.0, pltpu.roll(u, L - 1, 0))
        return prev * s_ref[0:1, :] + u * s_ref[1:2, :] + nxt * s_ref[2:3, :]

    def long_conv(z, order):
        spec = jnp.dot(fwd_ref[...], z.astype(BF16), preferred_element_type=F32)
        zc, zs = spec[:L], spec[L:]
        hc, hs = hc_ref[order], hs_ref[order]
        yc = zc * hc - jnp.where(first, 0.0, zs * hs)
        ys = jnp.where(first, zs * hs, zc * hs + zs * hc)
        y = jnp.dot(inv_ref[...], jnp.concatenate([yc, ys], axis=0).astype(BF16),
                    preferred_element_type=F32)
        return y + z * bias_ref[order:order + 1, :]

    z = short_conv(x1_ref, s1_ref) * long_conv(short_conv(v_ref, sv_ref), 0)
    o_ref[...] = (short_conv(x2_ref, s2_ref) * long_conv(z, 1)).astype(BF16)


def _hy_conv(proj, row0, nseq, L, tc, hy_short, hy_bias, hc, hs, fwd, inv):
    nj = HY_W // tc
    rb = row0 // L
    cb = C_HY // tc
    col = lambda k: pl.BlockSpec((L, tc), lambda b, j: (rb + b, cb + k * nj + j))
    sw = lambda k: pl.BlockSpec((3, tc), lambda b, j: (0, k * nj + j))
    spec3 = pl.BlockSpec((HY_ORDER, L, tc), lambda b, j: (0, 0, j))
    return pl.pallas_call(
        functools.partial(_hy_conv_kernel, L=L),
        grid=(nseq, nj),
        in_specs=[col(0), col(1), col(2), sw(0), sw(1), sw(2),
                  pl.BlockSpec((HY_ORDER, tc), lambda b, j: (0, j)), spec3, spec3,
                  pl.BlockSpec((2 * L, L), lambda b, j: (0, 0)),
                  pl.BlockSpec((L, 2 * L), lambda b, j: (0, 0))],
        out_specs=pl.BlockSpec((L, tc), lambda b, j: (b, j)),
        out_shape=jax.ShapeDtypeStruct((nseq * L, HY_W), BF16),
        compiler_params=_cparams(("arbitrary", "arbitrary")),
        name=f"hyena_conv_{L}",
    )(proj, proj, proj, hy_short, hy_short, hy_short, hy_bias, hc, hs, fwd, inv)


def _rope_tables(L):
    n_rows = L // GRID_W
    pos_r = jnp.repeat(jnp.arange(n_rows, dtype=F32), GRID_W)
    pos_c = jnp.tile(jnp.arange(GRID_W, dtype=F32), n_rows)
    nf = RET_DK // 4
    inv = ROPE_BASE ** (-jnp.arange(nf, dtype=F32) / nf)
    ar = pos_r[:, None] * inv[None, :]
    ac = pos_c[:, None] * inv[None, :]
    cos = jnp.concatenate([jnp.cos(ar), jnp.cos(ar), jnp.cos(ac), jnp.cos(ac)], axis=-1)
    sin = jnp.concatenate([-jnp.sin(ar), jnp.sin(ar), -jnp.sin(ac), jnp.sin(ac)], axis=-1)
    return jnp.tile(cos, (1, 2)), jnp.tile(sin, (1, 2))


def _ret_kernel(*refs, L, latent):
    if latent:
        (q_ref, k_ref, v_ref, g_ref, dec_ref, cos_ref, sin_ref, s0_ref, o_ref,
         acc_ref, qs_ref, ks_ref) = refs
    else:
        q_ref, k_ref, v_ref, g_ref, dec_ref, o_ref, sn_ref, acc_ref, qs_ref, ks_ref = refs
    C = RET_CHUNK
    n = L // C
    lane = lax.broadcasted_iota(jnp.int32, (1, 2 * RET_DK), 1)
    q = q_ref[...]
    k = k_ref[...] * (RET_DK ** -0.5)
    if latent:
        nf = RET_DK // 4
        lo = (lane % (2 * nf)) < nf

        def rope(x):
            swapped = jnp.where(lo, pltpu.roll(x, 2 * RET_DK - nf, 1), pltpu.roll(x, nf, 1))
            return x * cos_ref[...] + swapped * sin_ref[...]

        q, k = rope(q), rope(k)
    qs_ref[...] = q
    ks_ref[...] = k
    ri = lax.broadcasted_iota(jnp.int32, (C, C), 0)
    ci = lax.broadcasted_iota(jnp.int32, (C, C), 1)
    rel = (ri - ci).astype(F32)
    rowf = ri.astype(F32)

    chains = [(h, d) for h in range(2) for d in range(2)]
    heads, vcols, dmats, q_decs, k_decs, c_decs, states = [], [], [], [], [], [], []
    for h, d in chains:
        dec = dec_ref[d, 2 * pl.program_id(1) + h]
        lg = jnp.minimum(dec, 0.0) - jnp.log1p(jnp.exp(-jnp.abs(dec)))
        if d == 0:
            dmats.append(jnp.where(rel >= 0, jnp.exp(rel * lg), 0.0))
            q_decs.append(jnp.exp((rowf + 1.0) * lg))
            k_decs.append(jnp.exp((C - 1.0 - rowf) * lg))
        else:
            dmats.append(jnp.where(rel <= 0, jnp.exp(-rel * lg), 0.0))
            q_decs.append(jnp.exp((C - rowf) * lg))
            k_decs.append(jnp.exp(rowf * lg))
        c_decs.append(jnp.exp(C * lg))
        heads.append((lane // RET_DK) == h)
        vcols.append(slice(h * RET_DV, (h + 1) * RET_DV))
        if latent:
            s_init = s0_ref[0, d, h]
            zero = jnp.zeros_like(s_init)
            states.append(jnp.concatenate([s_init, zero] if h == 0 else [zero, s_init], axis=0))
        else:
            states.append(jnp.zeros((2 * RET_DK, RET_DV), F32))

    nt = lambda a, b: lax.dot_general(a, b, (((1,), (1,)), ((), ())), preferred_element_type=F32)
    tn = lambda a, b: lax.dot_general(a, b, (((0,), (0,)), ((), ())), preferred_element_type=F32)
    nn = lambda a, b: jnp.dot(a, b, preferred_element_type=F32)
    for i in range(n):
        rows = [pl.ds((i if d == 0 else n - 1 - i) * C, C) for _, d in chains]
        qc = [jnp.where(hd, qs_ref[r, :], 0.0) for hd, r in zip(heads, rows)]
        kc = [ks_ref[r, :] for r in rows]
        vc = [v_ref[r, vc_].astype(BF16) for r, vc_ in zip(rows, vcols)]
        sc = [nt(q.astype(BF16), k.astype(BF16)) * dm for q, k, dm in zip(qc, kc, dmats)]
        qs = [nn((q * qd).astype(BF16), s.astype(BF16)) for q, qd, s in zip(qc, q_decs, states)]
        kv = [tn((k * kd).astype(BF16), v) for k, kd, v in zip(kc, k_decs, vc)]
        o = [nn(s_.astype(BF16), v) + q_ for s_, v, q_ in zip(sc, vc, qs)]
        for (h, d), r, vc_, o_ in zip(chains, rows, vcols, o):
            acc_ref[d, r, vc_] = o_
        states = [s * cd + k_ for s, cd, k_ in zip(states, c_decs, kv)]
    if not latent:
        for (h, d), s in zip(chains, states):
            sn_ref[0, d, h] = s[h * RET_DK:(h + 1) * RET_DK]

    for h in range(2):
        vcol = slice(h * RET_DV, (h + 1) * RET_DV)
        o = acc_ref[0, :, vcol] + acc_ref[1, :, vcol]
        o = o * lax.rsqrt(jnp.mean(o * o, axis=-1, keepdims=True) + EPS)
        g = g_ref[:, vcol]
        o_ref[:, vcol] = (o * (g * jax.nn.sigmoid(g))).astype(BF16)


def _retention(proj, row0, nseq, L, latent, dec, s0=None):
    rb = row0 // L
    npair = RET_HEADS // 2
    qk = lambda c0: pl.BlockSpec((L, 2 * RET_DK), lambda b, p: (rb + b, c0 // (2 * RET_DK) + p))
    vg = lambda c0: pl.BlockSpec((L, 2 * RET_DV), lambda b, p: (rb + b, c0 // (2 * RET_DV) + p))
    in_specs = [qk(C_RQ), qk(C_RK), vg(C_RV), vg(C_RG),
                pl.BlockSpec((2, RET_HEADS, 1, LANES), lambda b, p: (0, 0, 0, 0))]
    args = [proj, proj, proj, proj, dec]
    out_specs = [pl.BlockSpec((L, 2 * RET_DV), lambda b, p: (b, p))]
    out_shape = [jax.ShapeDtypeStruct((nseq * L, RET_W), BF16)]
    st_spec = pl.BlockSpec((1, 2, 2, RET_DK, RET_DV), lambda b, p: (b, 0, p, 0, 0))
    if latent:
        cos, sin = _rope_tables(L)
        tab = pl.BlockSpec((L, 2 * RET_DK), lambda b, p: (0, 0))
        in_specs += [tab, tab, st_spec]
        args += [cos, sin, s0]
    else:
        out_specs.append(st_spec)
        out_shape.append(jax.ShapeDtypeStruct((nseq, 2, RET_HEADS, RET_DK, RET_DV), F32))
    return pl.pallas_call(
        functools.partial(_ret_kernel, L=L, latent=latent),
        grid=(nseq, npair),
        in_specs=in_specs, out_specs=out_specs, out_shape=out_shape,
        scratch_shapes=[pltpu.VMEM((2, L, 2 * RET_DV), F32),
                        pltpu.VMEM((L, 2 * RET_DK), F32),
                        pltpu.VMEM((L, 2 * RET_DK), F32)],
        compiler_params=_cparams(("arbitrary", "arbitrary")),
        name=f"retention_{L}",
    )(*args)


DN_C = 128
DN_HP_CTX = 4
DN_HP_LAT = 2
DN_GROUP = 2


def _dn_gate_params(a_log, dt_bias):
    gp = jnp.stack([a_log.reshape(-1), dt_bias.reshape(-1)], axis=0)
    return jnp.pad(gp, ((0, 0), (0, LANES - 2 * DN_HEADS)))


def _dot16(a, b):
    return jnp.dot(a.astype(BF16), b.astype(BF16), preferred_element_type=F32)


def _dot16_nt(a, b):
    return lax.dot_general(a.astype(BF16), b.astype(BF16), (((1,), (1,)), ((), ())),
                           preferred_element_type=F32)


def _dot16_tn(a, b):
    return lax.dot_general(a.astype(BF16), b.astype(BF16), (((0,), (0,)), ((), ())),
                           preferred_element_type=F32)


def _split3(x):
    p1 = x.astype(BF16)
    r = x - p1.astype(F32)
    p2 = r.astype(BF16)
    return p1, p2, (r - p2.astype(F32)).astype(BF16)


def _dot3(parts, rhs16):
    return sum(jnp.dot(p, rhs16, preferred_element_type=F32) for p in parts)


def _dot3_rhs(lhs16, parts):
    return sum(jnp.dot(lhs16, p, preferred_element_type=F32) for p in parts)


def _block_rows(c, size):
    if isinstance(c, int):
        return pl.ds(c * size, size)
    return pl.ds(pl.multiple_of(c * size, size), size)


def _dn_kernel(*refs, L, latent, hp):
    (q_ref, k_ref, v_ref, z_ref, ab_ref, cq_ref, ck_ref, cv_ref, gp_ref, nw_ref) = refs[:10]
    if latent:
        s0_ref, o_ref = refs[10:12]
    else:
        o_ref, sn_ref = refs[10:12]
    (qs_ref, ks_ref, vs_ref, beta_ref, g_ref, u_ref, acc_ref,
     w_ref, at_ref, qd_ref, kdt_ref, egl_ref) = refs[12:]
    C = DN_C
    n = L // C
    HP = hp
    row = lax.broadcasted_iota(jnp.int32, (L, 1), 0)
    first = row == 0
    last = row == L - 1

    def conv_silu(u_ref, c_ref, cols):
        u = u_ref[:, cols]
        prev = jnp.where(first, 0.0, pltpu.roll(u, 1, 0))
        nxt = jnp.where(last, 0.0, pltpu.roll(u, L - 1, 0))
        y = prev * c_ref[0:1, cols] + u * c_ref[1:2, cols] + nxt * c_ref[2:3, cols]
        return y * jax.nn.sigmoid(y)

    def l2n(x):
        return x * lax.rsqrt(jnp.sum(x * x, axis=-1, keepdims=True) + EPS)

    for hh in range(HP):
        cols = slice(hh * DN_DK, (hh + 1) * DN_DK)
        qs_ref[hh] = l2n(conv_silu(q_ref, cq_ref, cols)) * (DN_DK ** -0.5)
        ks_ref[hh] = l2n(conv_silu(k_ref, ck_ref, cols))
        vs_ref[hh] = conv_silu(v_ref, cv_ref, cols)

    ri = lax.broadcasted_iota(jnp.int32, (C, C), 0)
    ci = lax.broadcasted_iota(jnp.int32, (C, C), 1)
    rx = ri ^ ci
    eye = (ri == ci).astype(F32)

    ab = ab_ref[...]
    lane = lax.broadcasted_iota(jnp.int32, (1, LANES), 1)
    x = ab + gp_ref[1:2, :]
    g_all = -jnp.exp(gp_ref[0:1, :]) * (jnp.maximum(x, 0.0) + jnp.log1p(jnp.exp(-jnp.abs(x))))
    gates3 = _split3(jnp.where(lane < 2 * DN_HEADS, g_all, jax.nn.sigmoid(ab)))
    for hh in range(HP):
        head = HP * pl.program_id(1) + hh
        for d in range(2):
            sel_g = (ri == d * DN_HEADS + head).astype(BF16)
            sel_b = (ri == (2 + d) * DN_HEADS + head).astype(BF16)
            g_ref[2 * hh + d] = _dot3(gates3, sel_g)
            beta_ref[2 * hh + d] = _dot3(gates3[:2], sel_b)

    incl = [ri >= ci, ri <= ci]
    strict = [ri > ci, ri < ci]
    half_bit = [ri, ci]

    def quadrant(d, b):
        return (rx >= b // 2) & (rx < b) & ((half_bit[d] & (b // 2)) != 0)

    def prepare(insts):
        each = lambda fn, *cols: [fn(*a) for a in zip(*cols)]
        hs = [hh for hh, _, _ in insts]
        ds = [d for _, _, d in insts]
        hds = [2 * hh + d for hh, _, d in insts]
        rows = [_block_rows(c, C) for _, c, _ in insts]
        kc = each(lambda hh, r: ks_ref[hh, r, :], hs, rows)
        qc = each(lambda hh, r: qs_ref[hh, r, :], hs, rows)
        beta = each(lambda hd, r: beta_ref[hd, r, :], hds, rows)
        gc = each(lambda d, hd, r: _dot3_rhs(incl[d].astype(BF16), _split3(g_ref[hd, r, :])),
                  ds, hds, rows)
        decay = each(lambda d, g: jnp.where(incl[d], jnp.exp(jnp.where(incl[d], g - g.T, 0.0)), 0.0),
                     ds, gc)
        kb = each(lambda k, b_: k * b_, kc, beta)
        kk = each(_dot16_nt, kb, kc)
        a_mat = each(lambda d, x, dc: jnp.where(strict[d], x * dc, 0.0), ds, kk, decay)
        inv = each(lambda d, a: eye - jnp.where(quadrant(d, 2), a, 0.0), ds, a_mat)
        b = 4
        while b <= C:
            off = each(lambda d, a: jnp.where(quadrant(d, b), a, 0.0), ds, a_mat)
            t = each(_dot16, inv, off)
            t = each(_dot16, t, inv)
            inv = each(lambda x, y: x - y, inv, t)
            b *= 2
        egc = [jnp.exp(g) for g in gc]
        rhs = each(lambda hh, r, b_, k, e: jnp.concatenate([vs_ref[hh, r, :] * b_, k * e], axis=1),
                   hs, rows, beta, kb, egc)
        sol = each(_dot16, inv, rhs)
        qk = each(_dot16_nt, qc, kc)
        for (_, c, d), hd, r, s, a, dc, g, e, q, k in zip(insts, hds, rows, sol, qk, decay, gc, egc,
                                                         qc, kc):
            g_last = g[C - 1:C, :] if d == 0 else g[0:1, :]
            u_ref[hd, r, :] = s[:, :DN_DV]
            w_ref[hd, r, :] = s[:, DN_DV:].astype(BF16)
            at_ref[hd, r, :] = (a * dc).astype(BF16)
            qd_ref[hd, r, :] = (q * e).astype(BF16)
            kdt_ref[hd, r, :] = (k * jnp.exp(g_last - g)).T.astype(BF16)
            egl_ref[hd, _block_rows(c, 8), :] = jnp.broadcast_to(jnp.exp(g_last), (8, LANES))

    group = min(n, DN_GROUP)

    def prepare_group(j, carry):
        for h0 in range(0, HP, 2):
            prepare([(hh, group * j + t, d) for hh in (h0, h0 + 1) for t in range(group)
                     for d in range(2)])
        return carry

    if n == group:
        prepare_group(0, 0)
    else:
        lax.fori_loop(0, n // group, prepare_group, 0)

    chains = [(hh, d) for hh in range(HP) for d in range(2)]

    def advance(i, states):
        cs = [i if d == 0 else n - 1 - i for _, d in chains]
        rows = [_block_rows(c, C) for c in cs]
        hds = [2 * hh + d for hh, d in chains]
        dot = lambda a, b: jnp.dot(a, b, preferred_element_type=F32)
        s16 = [s.astype(BF16) for s in states]
        ws = [dot(w_ref[hd, r, :], s) for hd, r, s in zip(hds, rows, s16)]
        qs = [dot(qd_ref[hd, r, :], s) for hd, r, s in zip(hds, rows, s16)]
        v16 = [(u_ref[hd, r, :] - w).astype(BF16) for hd, r, w in zip(hds, rows, ws)]
        av = [dot(at_ref[hd, r, :], v) for hd, r, v in zip(hds, rows, v16)]
        kv = [dot(kdt_ref[hd, r, :], v) for hd, r, v in zip(hds, rows, v16)]
        new_states = []
        for hd, r, c, s, q, a, k in zip(hds, rows, cs, states, qs, av, kv):
            acc_ref[hd, r, :] = q + a
            new_states.append(s * egl_ref[hd, _block_rows(c, 8), :][0:1, :] + k)
        return tuple(new_states)

    if latent:
        init = tuple(s0_ref[0, d, hh] for hh, d in chains)
    else:
        init = (jnp.zeros((DN_DK, DN_DV), F32),) * len(chains)
    if n == 2:
        fin = advance(1, advance(0, init))
    else:
        fin = lax.fori_loop(0, n, advance, init)
    if not latent:
        for (hh, d), s in zip(chains, fin):
            sn_ref[0, d, hh] = s

    for hh in range(HP):
        cols = slice(hh * DN_DV, (hh + 1) * DN_DV)
        o = acc_ref[2 * hh] + acc_ref[2 * hh + 1]
        o = o * lax.rsqrt(jnp.mean(o * o, axis=-1, keepdims=True) + EPS) * nw_ref[...]
        z = z_ref[:, cols]
        o_ref[:, cols] = (o * (z * jax.nn.sigmoid(z))).astype(BF16)


def _deltanet(proj, ab, row0, nseq, L, latent, hp, dn_conv, gp, dn_norm, s0=None):
    rb = row0 // L
    wide = hp * DN_DK
    cb = C_DQKV // wide
    ng = DN_HEADS // hp
    col = lambda k: pl.BlockSpec((L, wide), lambda b, h: (rb + b, cb + k * ng + h))
    cw = lambda k: pl.BlockSpec((3, wide), lambda b, h: (0, k * ng + h))
    in_specs = [col(0), col(1), col(2),
                pl.BlockSpec((L, wide), lambda b, h: (rb + b, C_DZ // wide + h)),
                pl.BlockSpec((L, LANES), lambda b, h: (rb + b, 0)),
                cw(0), cw(1), cw(2),
                pl.BlockSpec((2, LANES), lambda b, h: (0, 0)),
                pl.BlockSpec((1, DN_DV), lambda b, h: (0, 0))]
    args = [proj, proj, proj, proj, ab, dn_conv, dn_conv, dn_conv, gp, dn_norm.reshape(1, DN_DV)]
    out_specs = [pl.BlockSpec((L, wide), lambda b, h: (b, h))]
    out_shape = [jax.ShapeDtypeStruct((nseq * L, DN_W), BF16)]
    st_spec = pl.BlockSpec((1, 2, hp, DN_DK, DN_DV), lambda b, h: (b, 0, h, 0, 0))
    if latent:
        in_specs.append(st_spec)
        args.append(s0)
    else:
        out_specs.append(st_spec)
        out_shape.append(jax.ShapeDtypeStruct((nseq, 2, DN_HEADS, DN_DK, DN_DV), F32))
    nslot = 2 * hp
    return pl.pallas_call(
        functools.partial(_dn_kernel, L=L, latent=latent, hp=hp),
        grid=(nseq, ng),
        in_specs=in_specs, out_specs=out_specs, out_shape=out_shape,
        scratch_shapes=([pltpu.VMEM((hp, L, DN_DK), F32)] * 3
                        + [pltpu.VMEM((nslot, L, DN_DV), F32)] * 4
                        + [pltpu.VMEM((nslot, L, DN_DV), BF16)] * 4
                        + [pltpu.VMEM((nslot, 8 * (L // DN_C), LANES), F32)]),
        compiler_params=_cparams(("arbitrary", "arbitrary")),
        name=f"deltanet_{L}",
    )(*args)


def _cast_pad_kernel(x_ref, o_ref):
    n = x_ref.shape[-1]
    o_ref[:, :n] = x_ref[...].astype(BF16)
    if o_ref.shape[-1] > n:
        o_ref[:, n:] = jnp.zeros((o_ref.shape[0], o_ref.shape[-1] - n), BF16)


def _cast_cols(w, width, n_blocks, out_width, tk, name):
    depth, k, _ = w.shape
    return pl.pallas_call(
        _cast_pad_kernel,
        grid=(n_blocks, depth, k // tk),
        in_specs=[pl.BlockSpec((None, tk, width), lambda b, l, i: (l, i, b))],
        out_specs=pl.BlockSpec((None, None, tk, out_width), lambda b, l, i: (b, l, i, 0)),
        out_shape=jax.ShapeDtypeStruct((n_blocks, depth, k, out_width), BF16),
        compiler_params=_cparams(("arbitrary", "arbitrary", "arbitrary")),
        name=name,
    )(w)


def _prep_weights(w_in, p_ret, p_hy, p_dn, w_o, w_up, ffn_conv, w_down):
    w_a = _cast_cols(w_in, TN_WIDE, C_MG // TN_WIDE, TN_WIDE, D_MODEL, "cast_w_in")
    w_g = w_in[:, :, O_MG:].astype(BF16)
    w_ab = jnp.pad(w_in[:, :, O_DA:O_MG].astype(BF16), ((0, 0), (0, 0), (0, LANES - (O_MG - O_DA))))
    padc = ((0, 0), (0, 0), (0, FFN_PAD - FFN_DIM))
    wu_gate = wu_val = _cast_cols(w_up, FFN_DIM, 2, FFN_PAD, 256, "cast_w_up")
    cw = jnp.concatenate([jnp.pad(ffn_conv[:, :, :FFN_DIM], padc),
                          jnp.pad(ffn_conv[:, :, FFN_DIM:], padc)], axis=2)
    wd = jnp.pad(w_down.astype(BF16), ((0, 0), (0, FFN_PAD - FFN_DIM), (0, 0)))
    return (w_a, w_g, w_ab, p_ret.astype(BF16), p_hy.astype(BF16), p_dn.astype(BF16),
            w_o.astype(BF16), wu_gate, wu_val, cw, wd)


def kernel(x_prompt, x_sample, state_ret, state_dn, c, c_ctx, w_ada, b_ada, norm1, w_in, ret_decay, hy_short, hy_w1, hy_b1, hy_freq1, hy_w2, hy_b2, hy_freq2, hy_w3, hy_bias, dn_conv, dn_a_log, dn_dt_bias, dn_norm, p_ret, p_hy, p_dn, w_o, norm2, w_up, ffn_conv, w_down, norm_f):
    x = jnp.concatenate([x_prompt.reshape(MP, D_MODEL), x_sample.reshape(MS, D_MODEL)], axis=0)
    cond = jnp.concatenate([c_ctx[None], c, jnp.zeros((3, D_MODEL), F32)], axis=0)
    mod_all = _adaln(cond, w_ada, b_ada)
    new_ret, new_dn = [], []
    dft_p, dft_s = _dft_mats(SEQ), _dft_mats(DEC_SEQ)
    (w_a, w_g, w_ab, pr, ph, pd, wo, wu_gate, wu_val, cw, wd) = _prep_weights(
        w_in, p_ret, p_hy, p_dn, w_o, w_up, ffn_conv, w_down)
    for l in range(DEPTH):
        mod = mod_all[l].reshape(8, 1, N_MOD * D_MODEL)
        proj, ab = _proj_in(x, norm1[l].reshape(1, D_MODEL), mod, l, w_a, w_g, w_ab)

        dec = jnp.broadcast_to(ret_decay[l][:, :, None, None], (2, RET_HEADS, 1, LANES))
        rp, s_r = _retention(proj, 0, BATCH, SEQ, False, dec)
        (rs,) = _retention(proj, MP, DEC_BATCH, DEC_SEQ, True, dec, state_ret[:, l])
        filt = (hy_w1[l], hy_b1[l], hy_freq1[l], hy_w2[l], hy_b2[l], hy_freq2[l], hy_w3[l])
        hcp, hsp = _hy_filter(SEQ, HY_TC_P, dft_p[0], *filt)
        hcs, hss = _hy_filter(DEC_SEQ, HY_TC_S, dft_s[0], *filt)
        hp = _hy_conv(proj, 0, BATCH, SEQ, HY_TC_P, hy_short[l], hy_bias[l], hcp, hsp, *dft_p)
        hs = _hy_conv(proj, MP, DEC_BATCH, DEC_SEQ, HY_TC_S, hy_short[l], hy_bias[l], hcs, hss, *dft_s)
        gp = _dn_gate_params(dn_a_log[l], dn_dt_bias[l])
        dp, s_d = _deltanet(proj, ab, 0, BATCH, SEQ, False, DN_HP_CTX, dn_conv[l], gp, dn_norm[l])
        (ds,) = _deltanet(proj, ab, MP, DEC_BATCH, DEC_SEQ, True, DN_HP_LAT, dn_conv[l], gp, dn_norm[l],
                          state_dn[:, l])
        new_ret.append(s_r)
        new_dn.append(s_d)
        mix = _merge((rp, hp, dp), (rs, hs, ds), l, pr, ph, pd, proj)
        x = _resid_matmul(mix, l, wo, x, mod, 2, TN_WIDE, "attn_out")
        act = _ffn_up(x, norm2[l].reshape(1, D_MODEL), mod, l, wu_gate, wu_val, cw)
        x = _resid_matmul(act, l, wd, x, mod, 5, TN, "ffn_down")
    nf = norm_f.reshape(1, D_MODEL)
    y_prompt = _final_norm(x, nf, 0, MP).reshape(BATCH, SEQ, D_MODEL)
    y_sample = _final_norm(x, nf, MP, MS).reshape(DEC_BATCH, DEC_SEQ, D_MODEL)
    return (y_prompt, y_sample, jnp.stack(new_ret, axis=1), jnp.stack(new_dn, axis=1))
```

```python
import functools
import math

import jax
import jax.numpy as jnp
import numpy as np
from jax import lax
from jax.experimental import pallas as pl
from jax.experimental.pallas import tpu as pltpu

F32 = jnp.float32
BF16 = jnp.bfloat16

D_MODEL = 2048
BATCH = 32
SEQ = 256
DEPTH = 2
DEC_BATCH = 4
DEC_SEQ = 1024
GRID_W = 64
EPS = 1e-6
ROPE_BASE = 10000.0
RET_HEADS = 8
RET_DK = 64
RET_DV = 128
RET_QK = RET_HEADS * RET_DK
RET_W = RET_HEADS * RET_DV
RET_CHUNK = 128
HY_W = 1024
HY_ORDER = 2
HY_EMB = 33
HY_HID = 64
HY_TARGET = 1e-2
HY_FAST = 0.3
HY_SLOW = 1.5
DN_HEADS = 8
DN_DK = 128
DN_DV = 128
DN_QK = DN_HEADS * DN_DK
DN_W = DN_HEADS * DN_DV
DN_CHUNK = 64
N_BRANCH = 3
FFN_DIM = 5504
N_MOD = 6

MP = BATCH * SEQ
MS = DEC_BATCH * DEC_SEQ
M_TOK = MP + MS

LANES = 128
TM = 1024
TN = 512
TN_WIDE = 1024
FFN_PAD = 5632
HY_TC_P = 1024
HY_TC_S = 256
VMEM_LIMIT = 56 * 1024 * 1024

C_RQ, C_RK, C_RV, C_RG = 0, 512, 1024, 2048
C_HY, C_DQKV, C_DZ, C_MG = 3072, 6144, 9216, 10240
N_MAIN = 16384
O_DA, O_MG = 10240, 10272


def _cparams(sem):
    return pltpu.CompilerParams(dimension_semantics=sem, vmem_limit_bytes=VMEM_LIMIT)


def _mod_row(i):
    return jnp.where(i < MP // TM, 0, 1 + (i - MP // TM) // (DEC_SEQ // TM))


def _mod_kernel(cond_ref, w_ref, b_ref, o_ref):
    c = cond_ref[...]
    s = (c * jax.nn.sigmoid(c)).astype(BF16)
    o_ref[0] = jnp.dot(s, w_ref[0].astype(BF16), preferred_element_type=F32) + b_ref[0]


def _adaln(cond, w_ada, b_ada):
    tn = 1024
    return pl.pallas_call(
        _mod_kernel,
        grid=(DEPTH, N_MOD * D_MODEL // tn),
        in_specs=[pl.BlockSpec((8, D_MODEL), lambda l, j: (0, 0)),
                  pl.BlockSpec((1, D_MODEL, tn), lambda l, j: (l, 0, j)),
                  pl.BlockSpec((1, 1, tn), lambda l, j: (l, 0, j))],
        out_specs=pl.BlockSpec((1, 8, tn), lambda l, j: (l, 0, j)),
        out_shape=jax.ShapeDtypeStruct((DEPTH, 8, N_MOD * D_MODEL), F32),
        compiler_params=_cparams(("arbitrary", "arbitrary")),
        name="adaln_mod",
    )(cond, w_ada, b_ada.reshape(DEPTH, 1, N_MOD * D_MODEL))


def _modulated_norm(x, nw, sc, sh):
    var = jnp.mean(x * x, axis=-1, keepdims=True)
    return (x * lax.rsqrt(var + EPS) * nw) * (1.0 + sc) + sh


def _proj_in_kernel(x_ref, nw_ref, sh_ref, sc_ref, w_ref, wab_ref, o_ref, ab_ref, h_ref):
    @pl.when(pl.program_id(1) == 0)
    def _():
        h = _modulated_norm(x_ref[...], nw_ref[...], sc_ref[0], sh_ref[0]).astype(BF16)
        h_ref[...] = h
        ab_ref[...] = jnp.dot(h, wab_ref[...], preferred_element_type=F32)

    o_ref[...] = jnp.dot(h_ref[...], w_ref[...], preferred_element_type=F32)


def _proj_in(x, nw, mod, l, w_main, w_ab):
    return pl.pallas_call(
        _proj_in_kernel,
        grid=(M_TOK // TM, N_MAIN // TN_WIDE),
        in_specs=[pl.BlockSpec((TM, D_MODEL), lambda i, j: (i, 0)),
                  pl.BlockSpec((1, D_MODEL), lambda i, j: (0, 0)),
                  pl.BlockSpec((1, 1, D_MODEL), lambda i, j: (_mod_row(i), 0, 0)),
                  pl.BlockSpec((1, 1, D_MODEL), lambda i, j: (_mod_row(i), 0, 1)),
                  pl.BlockSpec((None, D_MODEL, TN_WIDE), lambda i, j: (l, 0, j)),
                  pl.BlockSpec((None, D_MODEL, LANES), lambda i, j: (l, 0, 0))],
        out_specs=[pl.BlockSpec((TM, TN_WIDE), lambda i, j: (i, j)),
                   pl.BlockSpec((TM, LANES), lambda i, j: (i, 0))],
        out_shape=[jax.ShapeDtypeStruct((M_TOK, N_MAIN), F32),
                   jax.ShapeDtypeStruct((M_TOK, LANES), F32)],
        scratch_shapes=[pltpu.VMEM((TM, D_MODEL), BF16)],
        compiler_params=_cparams(("arbitrary", "arbitrary")),
        name="proj_in",
    )(x, nw, mod, mod, w_main, w_ab)


def _merge_kernel(rp_ref, hp_ref, dp_ref, rs_ref, hs_ref, ds_ref, pr_ref, ph_ref, pd_ref,
                  gr_ref, gh_ref, gd_ref, o_ref):
    def mix(r_ref, h_ref, d_ref):
        m = jax.nn.sigmoid(gr_ref[...]) * jnp.dot(r_ref[...], pr_ref[...], preferred_element_type=F32)
        m += jax.nn.sigmoid(gh_ref[...]) * jnp.dot(h_ref[...], ph_ref[...], preferred_element_type=F32)
        m += jax.nn.sigmoid(gd_ref[...]) * jnp.dot(d_ref[...], pd_ref[...], preferred_element_type=F32)
        o_ref[...] = m.astype(BF16)

    is_context = pl.program_id(0) < MP // TM

    @pl.when(is_context)
    def _():
        mix(rp_ref, hp_ref, dp_ref)

    @pl.when(jnp.logical_not(is_context))
    def _():
        mix(rs_ref, hs_ref, ds_ref)


def _merge(ctx, lat, l, p_ret, p_hy, p_dn, proj):
    g0 = C_MG // TN
    gstep = D_MODEL // TN
    npt = MP // TM
    cspec = pl.BlockSpec((TM, RET_W), lambda i, j: (jnp.minimum(i, npt - 1), 0))
    lspec = pl.BlockSpec((TM, RET_W), lambda i, j: (jnp.maximum(i - npt, 0), 0))
    pspec = pl.BlockSpec((None, RET_W, TN), lambda i, j: (l, 0, j))
    return pl.pallas_call(
        _merge_kernel,
        grid=(M_TOK // TM, D_MODEL // TN),
        in_specs=[cspec, cspec, cspec, lspec, lspec, lspec, pspec, pspec, pspec,
                  pl.BlockSpec((TM, TN), lambda i, j: (i, g0 + j)),
                  pl.BlockSpec((TM, TN), lambda i, j: (i, g0 + gstep + j)),
                  pl.BlockSpec((TM, TN), lambda i, j: (i, g0 + 2 * gstep + j))],
        out_specs=pl.BlockSpec((TM, TN), lambda i, j: (i, j)),
        out_shape=jax.ShapeDtypeStruct((M_TOK, D_MODEL), BF16),
        compiler_params=_cparams(("arbitrary", "arbitrary")),
        name="branch_merge",
    )(*ctx, *lat, p_ret, p_hy, p_dn, proj, proj, proj)


def _resid_kernel(a_ref, w_ref, x_ref, g_ref, o_ref):
    o_ref[...] = x_ref[...] + g_ref[0] * jnp.dot(a_ref[...], w_ref[...], preferred_element_type=F32)


def _resid_matmul(a, l, w, x, mod, gate_idx, tn, name):
    k = a.shape[1]
    gb = gate_idx * (D_MODEL // tn)
    return pl.pallas_call(
        _resid_kernel,
        grid=(M_TOK // TM, D_MODEL // tn),
        in_specs=[pl.BlockSpec((TM, k), lambda i, j: (i, 0)),
                  pl.BlockSpec((None, k, tn), lambda i, j: (l, 0, j)),
                  pl.BlockSpec((TM, tn), lambda i, j: (i, j)),
                  pl.BlockSpec((1, 1, tn), lambda i, j: (_mod_row(i), 0, gb + j))],
        out_specs=pl.BlockSpec((TM, tn), lambda i, j: (i, j)),
        out_shape=jax.ShapeDtypeStruct((M_TOK, D_MODEL), F32),
        compiler_params=_cparams(("arbitrary", "arbitrary")),
        name=name,
    )(a, w, x, mod)


PAD_ROWS = 8


def _ffn_up_kernel(x_ref, nw_ref, sh_ref, sc_ref, wa_ref, wb_ref, ca_ref, cb_ref, o_ref,
                   h_ref, ua_ref, ub_ref):
    i = pl.program_id(0)

    @pl.when(pl.program_id(1) == 0)
    def _():
        h_ref[...] = _modulated_norm(x_ref[...], nw_ref[...], sc_ref[0], sh_ref[0]).astype(BF16)
        zeros = jnp.zeros((PAD_ROWS, TN), F32)
        for ref in (ua_ref, ub_ref):
            ref[pl.ds(0, PAD_ROWS), :] = zeros
            ref[pl.ds(PAD_ROWS + TM, PAD_ROWS), :] = zeros

    h = h_ref[...]
    ua_ref[pl.ds(PAD_ROWS, TM), :] = jnp.dot(h, wa_ref[...], preferred_element_type=F32)
    ub_ref[pl.ds(PAD_ROWS, TM), :] = jnp.dot(h, wb_ref[...], preferred_element_type=F32)

    seq = jnp.where(i < MP // TM, SEQ, DEC_SEQ)
    pos = lax.broadcasted_iota(jnp.int32, (TM, 1), 0) & (seq - 1)
    has_prev = pos != 0
    has_next = pos != seq - 1

    def conv(u_ref, c_ref):
        prev = jnp.where(has_prev, u_ref[pl.ds(PAD_ROWS - 1, TM), :], 0.0)
        nxt = jnp.where(has_next, u_ref[pl.ds(PAD_ROWS + 1, TM), :], 0.0)
        mid = u_ref[pl.ds(PAD_ROWS, TM), :]
        return prev * c_ref[0:1, :] + mid * c_ref[1:2, :] + nxt * c_ref[2:3, :]

    ga = conv(ua_ref, ca_ref)
    gb = conv(ub_ref, cb_ref)
    o_ref[...] = (ga * jax.nn.sigmoid(ga) * gb).astype(BF16)


def _ffn_up(x, nw, mod, l, w_up, conv_w):
    nj = FFN_PAD // TN
    return pl.pallas_call(
        _ffn_up_kernel,
        grid=(M_TOK // TM, nj),
        in_specs=[pl.BlockSpec((TM, D_MODEL), lambda i, j: (i, 0)),
                  pl.BlockSpec((1, D_MODEL), lambda i, j: (0, 0)),
                  pl.BlockSpec((1, 1, D_MODEL), lambda i, j: (_mod_row(i), 0, 3)),
                  pl.BlockSpec((1, 1, D_MODEL), lambda i, j: (_mod_row(i), 0, 4)),
                  pl.BlockSpec((None, D_MODEL, TN), lambda i, j: (l, 0, j)),
                  pl.BlockSpec((None, D_MODEL, TN), lambda i, j: (l, 0, nj + j)),
                  pl.BlockSpec((None, 3, TN), lambda i, j: (l, 0, j)),
                  pl.BlockSpec((None, 3, TN), lambda i, j: (l, 0, nj + j))],
        out_specs=pl.BlockSpec((TM, TN), lambda i, j: (i, j)),
        out_shape=jax.ShapeDtypeStruct((M_TOK, FFN_PAD), BF16),
        scratch_shapes=[pltpu.VMEM((TM, D_MODEL), BF16),
                        pltpu.VMEM((TM + 2 * PAD_ROWS, TN), F32),
                        pltpu.VMEM((TM + 2 * PAD_ROWS, TN), F32)],
        compiler_params=_cparams(("arbitrary", "arbitrary")),
        name="ffn_up",
    )(x, nw, mod, mod, w_up, w_up, conv_w, conv_w)


def _final_norm_kernel(x_ref, w_ref, o_ref):
    x = x_ref[...]
    var = jnp.mean(x * x, axis=-1, keepdims=True)
    o_ref[...] = x * lax.rsqrt(var + EPS) * w_ref[...]


def _final_norm(x, w, row0, rows):
    rb = row0 // TM
    return pl.pallas_call(
        _final_norm_kernel,
        grid=(rows // TM,),
        in_specs=[pl.BlockSpec((TM, D_MODEL), lambda i: (rb + i, 0)),
                  pl.BlockSpec((1, D_MODEL), lambda i: (0, 0))],
        out_specs=pl.BlockSpec((TM, D_MODEL), lambda i: (i, 0)),
        out_shape=jax.ShapeDtypeStruct((rows, D_MODEL), F32),
        compiler_params=_cparams(("arbitrary",)),
        name=f"final_norm_{row0}",
    )(x, w)


def _dft_mats(L):
    n = 2 * L
    t = np.arange(L)
    ang = 2.0 * np.pi * ((t[:, None] * t[None, :]) % n) / n
    fc, fs = np.cos(ang), np.sin(ang)
    alt = (-1.0) ** t
    fs[0, :] = alt
    fwd = np.concatenate([fc, fs], axis=0)
    w = np.full((L, 1), 2.0)
    w[0] = 1.0
    gc = (fc * w / n).T
    gs = (fs * 2.0 / n).T
    gs[:, 0] = alt / n
    inv = np.concatenate([gc, gs], axis=1)
    return jnp.asarray(fwd, BF16), jnp.asarray(inv, BF16)


def _hy_feats(L):
    t = jnp.linspace(0.0, 1.0, L, dtype=F32)[:, None]
    bands = (HY_EMB - 1) // 2
    wpos = 2.0 * math.pi * jnp.arange(L, dtype=F32)[:, None] / L
    fr = jnp.linspace(1e-4, bands - 1, bands, dtype=F32)[None, :]
    feats = jnp.concatenate([t, jnp.cos(fr * wpos), -jnp.sin(fr * wpos)], axis=-1)
    return jnp.pad(feats, ((0, 0), (0, LANES - HY_EMB))), t


def _hy_filter_kernel(feats_ref, t_ref, w1_ref, b1_ref, f1_ref, w2_ref, b2_ref, f2_ref,
                      w3f_ref, w3b_ref, delta_ref, fwd_ref, hc_ref, hs_ref, hid_ref, *, L):
    @pl.when((pl.program_id(0) == 0) & (pl.program_id(1) == 0))
    def _():
        hid = jnp.dot(feats_ref[...].astype(BF16), w1_ref[...].astype(BF16), preferred_element_type=F32)
        hid = jnp.sin(f1_ref[...] * (hid + b1_ref[...]))
        hid = jnp.dot(hid.astype(BF16), w2_ref[...].astype(BF16), preferred_element_type=F32)
        hid_ref[...] = jnp.sin(f2_ref[...] * (hid + b2_ref[...])).astype(BF16)

    hid = hid_ref[...]
    win = jnp.exp(-t_ref[...] * delta_ref[...])

    def spectrum(w3_ref):
        h = jnp.dot(hid, w3_ref[...].astype(BF16), preferred_element_type=F32) * win
        h = h / (jnp.sum(jnp.abs(h), axis=0, keepdims=True) + EPS)
        return jnp.dot(fwd_ref[...], h.astype(BF16), preferred_element_type=F32)

    a = spectrum(w3f_ref)
    b = spectrum(w3b_ref)
    dc = lax.broadcasted_iota(jnp.int32, (L, 1), 0) == 0
    hc_ref[0] = a[:L] + b[:L]
    hs_ref[0] = a[L:] + jnp.where(dc, b[L:], -b[L:])


def _hy_filter(L, tc, fwd, w1, b1, f1, w2, b2, f2, w3):
    feats, t = _hy_feats(L)
    deltas = jnp.abs(jnp.linspace(math.log(HY_TARGET) / HY_FAST, math.log(HY_TARGET) / HY_SLOW,
                                  HY_W, dtype=F32))[None, :]
    w1p = jnp.pad(w1, ((0, LANES - HY_EMB), (0, 0)))
    nj = HY_W // tc
    full = lambda shape: pl.BlockSpec(shape, lambda o, j: (0,) * len(shape))
    out = jax.ShapeDtypeStruct((HY_ORDER, L, HY_W), F32)
    return pl.pallas_call(
        functools.partial(_hy_filter_kernel, L=L),
        grid=(HY_ORDER, nj),
        in_specs=[full((L, LANES)), full((L, 1)), full((LANES, HY_HID)), full((1, HY_HID)),
                  full((1, HY_HID)), full((HY_HID, HY_HID)), full((1, HY_HID)), full((1, HY_HID)),
                  pl.BlockSpec((HY_HID, tc), lambda o, j: (0, o * nj + j)),
                  pl.BlockSpec((HY_HID, tc), lambda o, j: (0, (HY_ORDER + o) * nj + j)),
                  pl.BlockSpec((1, tc), lambda o, j: (0, j)),
                  full((2 * L, L))],
        out_specs=[pl.BlockSpec((1, L, tc), lambda o, j: (o, 0, j)),
                   pl.BlockSpec((1, L, tc), lambda o, j: (o, 0, j))],
        out_shape=[out, out],
        scratch_shapes=[pltpu.VMEM((L, HY_HID), BF16)],
        compiler_params=_cparams(("arbitrary", "arbitrary")),
        name=f"hyena_filter_{L}",
    )(feats, t, w1p, b1.reshape(1, HY_HID), f1.reshape(1, HY_HID), w2, b2.reshape(1, HY_HID),
      f2.reshape(1, HY_HID), w3, w3, deltas, fwd)


def _hy_conv_kernel(v_ref, x1_ref, x2_ref, sv_ref, s1_ref, s2_ref, bias_ref, hc_ref, hs_ref,
                    fwd_ref, inv_ref, o_ref, *, L):
    row = lax.broadcasted_iota(jnp.int32, (L, 1), 0)
    first = row == 0
    last = row == L - 1

    def short_conv(u_ref, s_ref):
        u = u_ref[...]
        prev = jnp.where(first, 0.0, pltpu.roll(u, 1, 0))
        nxt = jnp.where(last, 0.0, pltpu.roll(u, L - 1, 0))
        return prev * s_ref[0:1, :] + u * s_ref[1:2, :] + nxt * s_ref[2:3, :]

    def long_conv(z, order):
        spec = jnp.dot(fwd_ref[...], z.astype(BF16), preferred_element_type=F32)
        zc, zs = spec[:L], spec[L:]
        hc, hs = hc_ref[order], hs_ref[order]
        yc = zc * hc - jnp.where(first, 0.0, zs * hs)
        ys = jnp.where(first, zs * hs, zc * hs + zs * hc)
        y = jnp.dot(inv_ref[...], jnp.concatenate([yc, ys], axis=0).astype(BF16),
                    preferred_element_type=F32)
        return y + z * bias_ref[order:order + 1, :]

    z = short_conv(x1_ref, s1_ref) * long_conv(short_conv(v_ref, sv_ref), 0)
    o_ref[...] = (short_conv(x2_ref, s2_ref) * long_conv(z, 1)).astype(BF16)


def _hy_conv(proj, row0, nseq, L, tc, hy_short, hy_bias, hc, hs, fwd, inv):
    nj = HY_W // tc
    rb = row0 // L
    cb = C_HY // tc
    col = lambda k: pl.BlockSpec((L, tc), lambda b, j: (rb + b, cb + k * nj + j))
    sw = lambda k: pl.BlockSpec((3, tc), lambda b, j: (0, k * nj + j))
    spec3 = pl.BlockSpec((HY_ORDER, L, tc), lambda b, j: (0, 0, j))
    return pl.pallas_call(
        functools.partial(_hy_conv_kernel, L=L),
        grid=(nseq, nj),
        in_specs=[col(0), col(1), col(2), sw(0), sw(1), sw(2),
                  pl.BlockSpec((HY_ORDER, tc), lambda b, j: (0, j)), spec3, spec3,
                  pl.BlockSpec((2 * L, L), lambda b, j: (0, 0)),
                  pl.BlockSpec((L, 2 * L), lambda b, j: (0, 0))],
        out_specs=pl.BlockSpec((L, tc), lambda b, j: (b, j)),
        out_shape=jax.ShapeDtypeStruct((nseq * L, HY_W), BF16),
        compiler_params=_cparams(("arbitrary", "arbitrary")),
        name=f"hyena_conv_{L}",
    )(proj, proj, proj, hy_short, hy_short, hy_short, hy_bias, hc, hs, fwd, inv)


def _rope_tables(L):
    n_rows = L // GRID_W
    pos_r = jnp.repeat(jnp.arange(n_rows, dtype=F32), GRID_W)
    pos_c = jnp.tile(jnp.arange(GRID_W, dtype=F32), n_rows)
    nf = RET_DK // 4
    inv = ROPE_BASE ** (-jnp.arange(nf, dtype=F32) / nf)
    ar = pos_r[:, None] * inv[None, :]
    ac = pos_c[:, None] * inv[None, :]
    cos = jnp.concatenate([jnp.cos(ar), jnp.cos(ar), jnp.cos(ac), jnp.cos(ac)], axis=-1)
    sin = jnp.concatenate([-jnp.sin(ar), jnp.sin(ar), -jnp.sin(ac), jnp.sin(ac)], axis=-1)
    return jnp.tile(cos, (1, 2)), jnp.tile(sin, (1, 2))


def _ret_kernel(*refs, L, latent):
    if latent:
        (q_ref, k_ref, v_ref, g_ref, dec_ref, cos_ref, sin_ref, s0_ref, o_ref,
         acc_ref, qs_ref, ks_ref) = refs
    else:
        q_ref, k_ref, v_ref, g_ref, dec_ref, o_ref, sn_ref, acc_ref, qs_ref, ks_ref = refs
    C = RET_CHUNK
    n = L // C
    lane = lax.broadcasted_iota(jnp.int32, (1, 2 * RET_DK), 1)
    q = q_ref[...]
    k = k_ref[...] * (RET_DK ** -0.5)
    if latent:
        nf = RET_DK // 4
        lo = (lane % (2 * nf)) < nf

        def rope(x):
            swapped = jnp.where(lo, pltpu.roll(x, 2 * RET_DK - nf, 1), pltpu.roll(x, nf, 1))
            return x * cos_ref[...] + swapped * sin_ref[...]

        q, k = rope(q), rope(k)
    qs_ref[...] = q
    ks_ref[...] = k
    ri = lax.broadcasted_iota(jnp.int32, (C, C), 0)
    ci = lax.broadcasted_iota(jnp.int32, (C, C), 1)
    rel = (ri - ci).astype(F32)
    rowf = ri.astype(F32)

    chains = [(h, d) for h in range(2) for d in range(2)]
    heads, vcols, dmats, q_decs, k_decs, c_decs, states = [], [], [], [], [], [], []
    for h, d in chains:
        dec = dec_ref[d, 2 * pl.program_id(1) + h]
        lg = jnp.minimum(dec, 0.0) - jnp.log1p(jnp.exp(-jnp.abs(dec)))
        if d == 0:
            dmats.append(jnp.where(rel >= 0, jnp.exp(rel * lg), 0.0))
            q_decs.append(jnp.exp((rowf + 1.0) * lg))
            k_decs.append(jnp.exp((C - 1.0 - rowf) * lg))
        else:
            dmats.append(jnp.where(rel <= 0, jnp.exp(-rel * lg), 0.0))
            q_decs.append(jnp.exp((C - rowf) * lg))
            k_decs.append(jnp.exp(rowf * lg))
        c_decs.append(jnp.exp(C * lg))
        heads.append((lane // RET_DK) == h)
        vcols.append(slice(h * RET_DV, (h + 1) * RET_DV))
        if latent:
            s_init = s0_ref[0, d, h]
            zero = jnp.zeros_like(s_init)
            states.append(jnp.concatenate([s_init, zero] if h == 0 else [zero, s_init], axis=0))
        else:
            states.append(jnp.zeros((2 * RET_DK, RET_DV), F32))

    nt = lambda a, b: lax.dot_general(a, b, (((1,), (1,)), ((), ())), preferred_element_type=F32)
    tn = lambda a, b: lax.dot_general(a, b, (((0,), (0,)), ((), ())), preferred_element_type=F32)
    nn = lambda a, b: jnp.dot(a, b, preferred_element_type=F32)
    for i in range(n):
        rows = [pl.ds((i if d == 0 else n - 1 - i) * C, C) for _, d in chains]
        qc = [jnp.where(hd, qs_ref[r, :], 0.0) for hd, r in zip(heads, rows)]
        kc = [ks_ref[r, :] for r in rows]
        vc = [v_ref[r, vc_].astype(BF16) for r, vc_ in zip(rows, vcols)]
        sc = [nt(q.astype(BF16), k.astype(BF16)) * dm for q, k, dm in zip(qc, kc, dmats)]
        qs = [nn((q * qd).astype(BF16), s.astype(BF16)) for q, qd, s in zip(qc, q_decs, states)]
        kv = [tn((k * kd).astype(BF16), v) for k, kd, v in zip(kc, k_decs, vc)]
        o = [nn(s_.astype(BF16), v) + q_ for s_, v, q_ in zip(sc, vc, qs)]
        for (h, d), r, vc_, o_ in zip(chains, rows, vcols, o):
            acc_ref[d, r, vc_] = o_
        states = [s * cd + k_ for s, cd, k_ in zip(states, c_decs, kv)]
    if not latent:
        for (h, d), s in zip(chains, states):
            sn_ref[0, d, h] = s[h * RET_DK:(h + 1) * RET_DK]

    for h in range(2):
        vcol = slice(h * RET_DV, (h + 1) * RET_DV)
        o = acc_ref[0, :, vcol] + acc_ref[1, :, vcol]
        o = o * lax.rsqrt(jnp.mean(o * o, axis=-1, keepdims=True) + EPS)
        g = g_ref[:, vcol]
        o_ref[:, vcol] = (o * (g * jax.nn.sigmoid(g))).astype(BF16)


def _retention(proj, row0, nseq, L, latent, dec, s0=None):
    rb = row0 // L
    npair = RET_HEADS // 2
    qk = lambda c0: pl.BlockSpec((L, 2 * RET_DK), lambda b, p: (rb + b, c0 // (2 * RET_DK) + p))
    vg = lambda c0: pl.BlockSpec((L, 2 * RET_DV), lambda b, p: (rb + b, c0 // (2 * RET_DV) + p))
    in_specs = [qk(C_RQ), qk(C_RK), vg(C_RV), vg(C_RG),
                pl.BlockSpec((2, RET_HEADS, 1, LANES), lambda b, p: (0, 0, 0, 0))]
    args = [proj, proj, proj, proj, dec]
    out_specs = [pl.BlockSpec((L, 2 * RET_DV), lambda b, p: (b, p))]
    out_shape = [jax.ShapeDtypeStruct((nseq * L, RET_W), BF16)]
    st_spec = pl.BlockSpec((1, 2, 2, RET_DK, RET_DV), lambda b, p: (b, 0, p, 0, 0))
    if latent:
        cos, sin = _rope_tables(L)
        tab = pl.BlockSpec((L, 2 * RET_DK), lambda b, p: (0, 0))
        in_specs += [tab, tab, st_spec]
        args += [cos, sin, s0]
    else:
        out_specs.append(st_spec)
        out_shape.append(jax.ShapeDtypeStruct((nseq, 2, RET_HEADS, RET_DK, RET_DV), F32))
    return pl.pallas_call(
        functools.partial(_ret_kernel, L=L, latent=latent),
        grid=(nseq, npair),
        in_specs=in_specs, out_specs=out_specs, out_shape=out_shape,
        scratch_shapes=[pltpu.VMEM((2, L, 2 * RET_DV), F32),
                        pltpu.VMEM((L, 2 * RET_DK), F32),
                        pltpu.VMEM((L, 2 * RET_DK), F32)],
        compiler_params=_cparams(("arbitrary", "arbitrary")),
        name=f"retention_{L}",
    )(*args)


DN_C = 128
DN_HP_CTX = 8
DN_HP_LAT = 2
DN_GROUP = 2


def _dn_gate_params(a_log, dt_bias):
    gp = jnp.stack([a_log.reshape(-1), dt_bias.reshape(-1)], axis=0)
    return jnp.pad(gp, ((0, 0), (0, LANES - 2 * DN_HEADS)))


def _dot16(a, b):
    return jnp.dot(a.astype(BF16), b.astype(BF16), preferred_element_type=F32)


def _dot16_nt(a, b):
    return lax.dot_general(a.astype(BF16), b.astype(BF16), (((1,), (1,)), ((), ())),
                           preferred_element_type=F32)


def _dot16_tn(a, b):
    return lax.dot_general(a.astype(BF16), b.astype(BF16), (((0,), (0,)), ((), ())),
                           preferred_element_type=F32)


def _split3(x):
    p1 = x.astype(BF16)
    r = x - p1.astype(F32)
    p2 = r.astype(BF16)
    return p1, p2, (r - p2.astype(F32)).astype(BF16)


def _dot3(parts, rhs16):
    return sum(jnp.dot(p, rhs16, preferred_element_type=F32) for p in parts)


def _dot3_rhs(lhs16, parts):
    return sum(jnp.dot(lhs16, p, preferred_element_type=F32) for p in parts)


def _block_rows(c, size):
    if isinstance(c, int):
        return pl.ds(c * size, size)
    return pl.ds(pl.multiple_of(c * size, size), size)


def _dn_kernel(*refs, L, latent, hp):
    (q_ref, k_ref, v_ref, z_ref, ab_ref, cq_ref, ck_ref, cv_ref, gp_ref, nw_ref) = refs[:10]
    if latent:
        s0_ref, o_ref = refs[10:12]
    else:
        o_ref, sn_ref = refs[10:12]
    (qs_ref, ks_ref, vs_ref, beta_ref, g_ref, u_ref, acc_ref,
     w_ref, at_ref, qd_ref, kdt_ref, egl_ref) = refs[12:]
    C = DN_C
    n = L // C
    HP = hp
    row = lax.broadcasted_iota(jnp.int32, (L, 1), 0)
    first = row == 0
    last = row == L - 1

    def conv_silu(u_ref, c_ref, cols):
        u = u_ref[:, cols]
        prev = jnp.where(first, 0.0, pltpu.roll(u, 1, 0))
        nxt = jnp.where(last, 0.0, pltpu.roll(u, L - 1, 0))
        y = prev * c_ref[0:1, cols] + u * c_ref[1:2, cols] + nxt * c_ref[2:3, cols]
        return y * jax.nn.sigmoid(y)

    def l2n(x):
        return x * lax.rsqrt(jnp.sum(x * x, axis=-1, keepdims=True) + EPS)

    for hh in range(HP):
        cols = slice(hh * DN_DK, (hh + 1) * DN_DK)
        qs_ref[hh] = l2n(conv_silu(q_ref, cq_ref, cols)) * (DN_DK ** -0.5)
        ks_ref[hh] = l2n(conv_silu(k_ref, ck_ref, cols))
        vs_ref[hh] = conv_silu(v_ref, cv_ref, cols)

    ri = lax.broadcasted_iota(jnp.int32, (C, C), 0)
    ci = lax.broadcasted_iota(jnp.int32, (C, C), 1)
    rx = ri ^ ci
    eye = (ri == ci).astype(F32)

    ab = ab_ref[...]
    lane = lax.broadcasted_iota(jnp.int32, (1, LANES), 1)
    x = ab + gp_ref[1:2, :]
    g_all = -jnp.exp(gp_ref[0:1, :]) * (jnp.maximum(x, 0.0) + jnp.log1p(jnp.exp(-jnp.abs(x))))
    gates3 = _split3(jnp.where(lane < 2 * DN_HEADS, g_all, jax.nn.sigmoid(ab)))
    for hh in range(HP):
        head = HP * pl.program_id(1) + hh
        for d in range(2):
            sel_g = (ri == d * DN_HEADS + head).astype(BF16)
            sel_b = (ri == (2 + d) * DN_HEADS + head).astype(BF16)
            g_ref[2 * hh + d] = _dot3(gates3, sel_g)
            beta_ref[2 * hh + d] = _dot3(gates3[:2], sel_b)

    incl = [ri >= ci, ri <= ci]
    strict = [ri > ci, ri < ci]
    half_bit = [ri, ci]

    def quadrant(d, b):
        return (rx >= b // 2) & (rx < b) & ((half_bit[d] & (b // 2)) != 0)

    def prepare(insts):
        each = lambda fn, *cols: [fn(*a) for a in zip(*cols)]
        hs = [hh for hh, _, _ in insts]
        ds = [d for _, _, d in insts]
        hds = [2 * hh + d for hh, _, d in insts]
        rows = [_block_rows(c, C) for _, c, _ in insts]
        kc = each(lambda hh, r: ks_ref[hh, r, :], hs, rows)
        qc = each(lambda hh, r: qs_ref[hh, r, :], hs, rows)
        beta = each(lambda hd, r: beta_ref[hd, r, :], hds, rows)
        gc = each(lambda d, hd, r: _dot3_rhs(incl[d].astype(BF16), _split3(g_ref[hd, r, :])),
                  ds, hds, rows)
        decay = each(lambda d, g: jnp.where(incl[d], jnp.exp(jnp.where(incl[d], g - g.T, 0.0)), 0.0),
                     ds, gc)
        kb = each(lambda k, b_: k * b_, kc, beta)
        kk = each(_dot16_nt, kb, kc)
        a_mat = each(lambda d, x, dc: jnp.where(strict[d], x * dc, 0.0), ds, kk, decay)
        inv = each(lambda d, a: eye - jnp.where(quadrant(d, 2), a, 0.0), ds, a_mat)
        b = 4
        while b <= C:
            off = each(lambda d, a: jnp.where(quadrant(d, b), a, 0.0), ds, a_mat)
            t = each(_dot16, inv, off)
            t = each(_dot16, t, inv)
            inv = each(lambda x, y: x - y, inv, t)
            b *= 2
        egc = [jnp.exp(g) for g in gc]
        rhs = each(lambda hh, r, b_, k, e: jnp.concatenate([vs_ref[hh, r, :] * b_, k * e], axis=1),
                   hs, rows, beta, kb, egc)
        sol = each(_dot16, inv, rhs)
        qk = each(_dot16_nt, qc, kc)
        for (_, c, d), hd, r, s, a, dc, g, e, q, k in zip(insts, hds, rows, sol, qk, decay, gc, egc,
                                                         qc, kc):
            g_last = g[C - 1:C, :] if d == 0 else g[0:1, :]
            u_ref[hd, r, :] = s[:, :DN_DV]
            w_ref[hd, r, :] = s[:, DN_DV:].astype(BF16)
            at_ref[hd, r, :] = (a * dc).astype(BF16)
            qd_ref[hd, r, :] = (q * e).astype(BF16)
            kdt_ref[hd, r, :] = (k * jnp.exp(g_last - g)).T.astype(BF16)
            egl_ref[hd, _block_rows(c, 8), :] = jnp.broadcast_to(jnp.exp(g_last), (8, LANES))

    group = min(n, DN_GROUP)

    def prepare_group(j, carry):
        for h0 in range(0, HP, 2):
            prepare([(hh, group * j + t, d) for hh in (h0, h0 + 1) for t in range(group)
                     for d in range(2)])
        return carry

    if n == group:
        prepare_group(0, 0)
    else:
        lax.fori_loop(0, n // group, prepare_group, 0)

    chains = [(hh, d) for hh in range(HP) for d in range(2)]

    def advance(i, states):
        cs = [i if d == 0 else n - 1 - i for _, d in chains]
        rows = [_block_rows(c, C) for c in cs]
        hds = [2 * hh + d for hh, d in chains]
        dot = lambda a, b: jnp.dot(a, b, preferred_element_type=F32)
        s16 = [s.astype(BF16) for s in states]
        ws = [dot(w_ref[hd, r, :], s) for hd, r, s in zip(hds, rows, s16)]
        qs = [dot(qd_ref[hd, r, :], s) for hd, r, s in zip(hds, rows, s16)]
        v16 = [(u_ref[hd, r, :] - w).astype(BF16) for hd, r, w in zip(hds, rows, ws)]
        av = [dot(at_ref[hd, r, :], v) for hd, r, v in zip(hds, rows, v16)]
        kv = [dot(kdt_ref[hd, r, :], v) for hd, r, v in zip(hds, rows, v16)]
        new_states = []
        for hd, r, c, s, q, a, k in zip(hds, rows, cs, states, qs, av, kv):
            acc_ref[hd, r, :] = q + a
            new_states.append(s * egl_ref[hd, _block_rows(c, 8), :][0:1, :] + k)
        return tuple(new_states)

    if latent:
        init = tuple(s0_ref[0, d, hh] for hh, d in chains)
    else:
        init = (jnp.zeros((DN_DK, DN_DV), F32),) * len(chains)
    if n == 2:
        fin = advance(1, advance(0, init))
    else:
        fin = lax.fori_loop(0, n, advance, init)
    if not latent:
        for (hh, d), s in zip(chains, fin):
            sn_ref[0, d, hh] = s

    for hh in range(HP):
        cols = slice(hh * DN_DV, (hh + 1) * DN_DV)
        o = acc_ref[2 * hh] + acc_ref[2 * hh + 1]
        o = o * lax.rsqrt(jnp.mean(o * o, axis=-1, keepdims=True) + EPS) * nw_ref[...]
        z = z_ref[:, cols]
        o_ref[:, cols] = (o * (z * jax.nn.sigmoid(z))).astype(BF16)


def _deltanet(proj, ab, row0, nseq, L, latent, hp, dn_conv, gp, dn_norm, s0=None):
    rb = row0 // L
    wide = hp * DN_DK
    cb = C_DQKV // wide
    ng = DN_HEADS // hp
    col = lambda k: pl.BlockSpec((L, wide), lambda b, h: (rb + b, cb + k * ng + h))
    cw = lambda k: pl.BlockSpec((3, wide), lambda b, h: (0, k * ng + h))
    in_specs = [col(0), col(1), col(2),
                pl.BlockSpec((L, wide), lambda b, h: (rb + b, C_DZ // wide + h)),
                pl.BlockSpec((L, LANES), lambda b, h: (rb + b, 0)),
                cw(0), cw(1), cw(2),
                pl.BlockSpec((2, LANES), lambda b, h: (0, 0)),
                pl.BlockSpec((1, DN_DV), lambda b, h: (0, 0))]
    args = [proj, proj, proj, proj, ab, dn_conv, dn_conv, dn_conv, gp, dn_norm.reshape(1, DN_DV)]
    out_specs = [pl.BlockSpec((L, wide), lambda b, h: (b, h))]
    out_shape = [jax.ShapeDtypeStruct((nseq * L, DN_W), BF16)]
    st_spec = pl.BlockSpec((1, 2, hp, DN_DK, DN_DV), lambda b, h: (b, 0, h, 0, 0))
    if latent:
        in_specs.append(st_spec)
        args.append(s0)
    else:
        out_specs.append(st_spec)
        out_shape.append(jax.ShapeDtypeStruct((nseq, 2, DN_HEADS, DN_DK, DN_DV), F32))
    nslot = 2 * hp
    return pl.pallas_call(
        functools.partial(_dn_kernel, L=L, latent=latent, hp=hp),
        grid=(nseq, ng),
        in_specs=in_specs, out_specs=out_specs, out_shape=out_shape,
        scratch_shapes=([pltpu.VMEM((hp, L, DN_DK), F32)] * 3
                        + [pltpu.VMEM((nslot, L, DN_DV), F32)] * 4
                        + [pltpu.VMEM((nslot, L, DN_DV), BF16)] * 4
                        + [pltpu.VMEM((nslot, 8 * (L // DN_C), LANES), F32)]),
        compiler_params=_cparams(("arbitrary", "arbitrary")),
        name=f"deltanet_{L}",
    )(*args)


def _prep_weights(w_in, p_ret, p_hy, p_dn, w_o, w_up, ffn_conv, w_down):
    w_main = jnp.concatenate([w_in[:, :, :O_DA].astype(BF16), w_in[:, :, O_MG:].astype(BF16)], axis=2)
    w_ab = jnp.pad(w_in[:, :, O_DA:O_MG].astype(BF16), ((0, 0), (0, 0), (0, LANES - (O_MG - O_DA))))
    padc = ((0, 0), (0, 0), (0, FFN_PAD - FFN_DIM))
    wu = jnp.concatenate([jnp.pad(w_up[:, :, :FFN_DIM].astype(BF16), padc),
                          jnp.pad(w_up[:, :, FFN_DIM:].astype(BF16), padc)], axis=2)
    cw = jnp.concatenate([jnp.pad(ffn_conv[:, :, :FFN_DIM], padc),
                          jnp.pad(ffn_conv[:, :, FFN_DIM:], padc)], axis=2)
    wd = jnp.pad(w_down.astype(BF16), ((0, 0), (0, FFN_PAD - FFN_DIM), (0, 0)))
    return (w_main, w_ab, p_ret.astype(BF16), p_hy.astype(BF16), p_dn.astype(BF16),
            w_o.astype(BF16), wu, cw, wd)


def kernel(x_prompt, x_sample, state_ret, state_dn, c, c_ctx, w_ada, b_ada, norm1, w_in, ret_decay, hy_short, hy_w1, hy_b1, hy_freq1, hy_w2, hy_b2, hy_freq2, hy_w3, hy_bias, dn_conv, dn_a_log, dn_dt_bias, dn_norm, p_ret, p_hy, p_dn, w_o, norm2, w_up, ffn_conv, w_down, norm_f):
    x = jnp.concatenate([x_prompt.reshape(MP, D_MODEL), x_sample.reshape(MS, D_MODEL)], axis=0)
    cond = jnp.concatenate([c_ctx[None], c, jnp.zeros((3, D_MODEL), F32)], axis=0)
    mod_all = _adaln(cond, w_ada, b_ada)
    new_ret, new_dn = [], []
    dft_p, dft_s = _dft_mats(SEQ), _dft_mats(DEC_SEQ)
    (w_main, w_ab, pr, ph, pd, wo, wu, cw, wd) = _prep_weights(
        w_in, p_ret, p_hy, p_dn, w_o, w_up, ffn_conv, w_down)
    for l in range(DEPTH):
        mod = mod_all[l].reshape(8, 1, N_MOD * D_MODEL)
        proj, ab = _proj_in(x, norm1[l].reshape(1, D_MODEL), mod, l, w_main, w_ab)

        dec = jnp.broadcast_to(ret_decay[l][:, :, None, None], (2, RET_HEADS, 1, LANES))
        rp, s_r = _retention(proj, 0, BATCH, SEQ, False, dec)
        (rs,) = _retention(proj, MP, DEC_BATCH, DEC_SEQ, True, dec, state_ret[:, l])
        filt = (hy_w1[l], hy_b1[l], hy_freq1[l], hy_w2[l], hy_b2[l], hy_freq2[l], hy_w3[l])
        hcp, hsp = _hy_filter(SEQ, HY_TC_P, dft_p[0], *filt)
        hcs, hss = _hy_filter(DEC_SEQ, HY_TC_S, dft_s[0], *filt)
        hp = _hy_conv(proj, 0, BATCH, SEQ, HY_TC_P, hy_short[l], hy_bias[l], hcp, hsp, *dft_p)
        hs = _hy_conv(proj, MP, DEC_BATCH, DEC_SEQ, HY_TC_S, hy_short[l], hy_bias[l], hcs, hss, *dft_s)
        gp = _dn_gate_params(dn_a_log[l], dn_dt_bias[l])
        dp, s_d = _deltanet(proj, ab, 0, BATCH, SEQ, False, DN_HP_CTX, dn_conv[l], gp, dn_norm[l])
        (ds,) = _deltanet(proj, ab, MP, DEC_BATCH, DEC_SEQ, True, DN_HP_LAT, dn_conv[l], gp, dn_norm[l],
                          state_dn[:, l])
        new_ret.append(s_r)
        new_dn.append(s_d)
        mix = _merge((rp, hp, dp), (rs, hs, ds), l, pr, ph, pd, proj)
        x = _resid_matmul(mix, l, wo, x, mod, 2, TN_WIDE, "attn_out")
        act = _ffn_up(x, norm2[l].reshape(1, D_MODEL), mod, l, wu, cw)
        x = _resid_matmul(act, l, wd, x, mod, 5, TN, "ffn_down")
    nf = norm_f.reshape(1, D_MODEL)
    y_prompt = _final_norm(x, nf, 0, MP).reshape(BATCH, SEQ, D_MODEL)
    y_sample = _final_norm(x, nf, MP, MS).reshape(DEC_BATCH, DEC_SEQ, D_MODEL)
    return (y_prompt, y_sample, jnp.stack(new_ret, axis=1), jnp.stack(new_dn, axis=1))
```
